```python
import jax, jax.numpy as jnp
from jax import lax
import numpy as np

D_MODEL = 1024
BATCH = 16
SEQ = 4096
DEPTH = 4

MLA_HEADS = 8
MLA_NOPE_DIM = 64
MLA_ROPE_DIM = 32
MLA_V_DIM = 64
MLA_Q_LORA = 256
MLA_KV_LORA = 128
ROPE_THETA = 10000.0
FOX_HEADS = 8
FOX_HEAD_DIM = 64
Q_BLOCK = 128
D_FF = 3584
N_EXPERTS = 8
TOP_K = 2
RMS_EPS = 1e-6

MLA_WIDTH = MLA_HEADS * MLA_V_DIM
FOX_WIDTH = FOX_HEADS * FOX_HEAD_DIM
MLA_QK_DIM = MLA_NOPE_DIM + MLA_ROPE_DIM
IN_WIDTHS = (MLA_Q_LORA, MLA_KV_LORA, MLA_ROPE_DIM, FOX_WIDTH, FOX_WIDTH, FOX_WIDTH, FOX_HEADS, D_MODEL, D_MODEL)
IN_COLS = sum(IN_WIDTHS)
N_DENSE = (DEPTH + 1) // 2
N_MOE = DEPTH // 2

kernel_name = "hybrid_mla_fox_gated_moe_adaln"


def rmsnorm(x, g):
    xf = x.astype(jnp.float32)
    y = xf * lax.rsqrt(jnp.mean(xf * xf, axis=-1, keepdims=True) + RMS_EPS)
    return (y * g.astype(jnp.float32)).astype(x.dtype)


def apply_rope(t, cos, sin):
    half = t.shape[-1] // 2
    t1, t2 = t[..., :half], t[..., half:]
    return jnp.concatenate([t1 * cos - t2 * sin, t1 * sin + t2 * cos], axis=-1)


def adaln(c, w, b):
    mod = jax.nn.silu(c) @ w + b
    shift, scale, gate = jnp.split(mod, 3, axis=-1)
    return shift[:, None, :], scale[:, None, :], gate[:, None, :]


def causal_block_attention(logits_fn, v):
    seq = v.shape[1]
    outs = []
    for q0 in range(0, seq, Q_BLOCK):
        q1 = q0 + Q_BLOCK
        logits = logits_fn(q0, q1)
        causal = (q0 + jnp.arange(Q_BLOCK))[:, None] >= jnp.arange(q1)[None, :]
        p = jax.nn.softmax(jnp.where(causal, logits, -jnp.inf), axis=-1)
        outs.append(jnp.einsum('bhqs,bshd->bqhd', p.astype(v.dtype), v[:, :q1]))
    return jnp.concatenate(outs, axis=1)


def hybrid_mixer(h, cos, sin, w_in, b_forget, q_norm_g, w_uq, kv_norm_g, w_ukv,
                 w_branch_mla, w_branch_fox, w_out):
    bsz, seq, _ = h.shape
    proj = h @ w_in
    offs = np.cumsum(IN_WIDTHS)[:-1].tolist()
    c_q, c_kv, k_r, fq, fk, fv, f_logit, g_a, g_b = jnp.split(proj, offs, axis=-1)

    q = (rmsnorm(c_q, q_norm_g) @ w_uq).reshape(bsz, seq, MLA_HEADS, MLA_QK_DIM)
    q_nope = q[..., :MLA_NOPE_DIM]
    q_rope = apply_rope(q[..., MLA_NOPE_DIM:], cos[:, :, None, :], sin[:, :, None, :])
    kv = (rmsnorm(c_kv, kv_norm_g) @ w_ukv).reshape(bsz, seq, MLA_HEADS, MLA_NOPE_DIM + MLA_V_DIM)
    k_nope, v_mla = kv[..., :MLA_NOPE_DIM], kv[..., MLA_NOPE_DIM:]
    k_rope = apply_rope(k_r, cos, sin)
    mla_scale = MLA_QK_DIM ** -0.5

    def mla_logits(q0, q1):
        s = jnp.einsum('bqhd,bshd->bhqs', q_nope[:, q0:q1], k_nope[:, :q1],
                       preferred_element_type=jnp.float32)
        s = s + jnp.einsum('bqhr,bsr->bhqs', q_rope[:, q0:q1], k_rope[:, :q1],
                           preferred_element_type=jnp.float32)
        return s * mla_scale

    y_mla = causal_block_attention(mla_logits, v_mla).reshape(bsz, seq, MLA_WIDTH)

    fq = fq.reshape(bsz, seq, FOX_HEADS, FOX_HEAD_DIM)
    fk = fk.reshape(bsz, seq, FOX_HEADS, FOX_HEAD_DIM)
    fv = fv.reshape(bsz, seq, FOX_HEADS, FOX_HEAD_DIM)
    log_f = jax.nn.log_sigmoid(f_logit.astype(jnp.float32) + b_forget.astype(jnp.float32))
    cum_log_f = jnp.transpose(jnp.cumsum(log_f, axis=1), (0, 2, 1))
    fox_scale = FOX_HEAD_DIM ** -0.5

    def fox_logits(q0, q1):
        s = jnp.einsum('bqhd,bshd->bhqs', fq[:, q0:q1], fk[:, :q1],
                       preferred_element_type=jnp.float32) * fox_scale
        return s + (cum_log_f[:, :, q0:q1, None] - cum_log_f[:, :, None, :q1])

    y_fox = causal_block_attention(fox_logits, fv).reshape(bsz, seq, FOX_WIDTH)

    merged = jax.nn.sigmoid(g_a) * (y_mla @ w_branch_mla) + jax.nn.sigmoid(g_b) * (y_fox @ w_branch_fox)
    return merged @ w_out


def swiglu(h, w_gate, w_up, w_down):
    return (jax.nn.silu(h @ w_gate) * (h @ w_up)) @ w_down


def moe_swiglu(h, w_router, w_gate, w_up, w_down):
    logits = jnp.einsum('bsd,de->bse', h, w_router, preferred_element_type=jnp.float32)
    top_v, top_i = lax.top_k(logits, TOP_K)
    top_w = jax.nn.softmax(top_v, axis=-1)
    combine = jnp.sum(jax.nn.one_hot(top_i, N_EXPERTS, dtype=jnp.float32) * top_w[..., None],
                      axis=-2).astype(h.dtype)
    y = jnp.zeros_like(h)
    for e in range(N_EXPERTS):
        y = y + combine[..., e:e + 1] * swiglu(h, w_gate[e], w_up[e], w_down[e])
    return y


def setup_inputs(seed: int = 0) -> dict:
    key = jax.random.key(seed)
    ks = iter(jax.random.split(key, 40))

    def nrm(shape, scale):
        return jax.random.normal(next(ks), shape, jnp.float32) * scale

    def gain(shape):
        return 1.0 + nrm(shape, 0.02)

    x = nrm((BATCH, SEQ, D_MODEL), 1.0)
    c = nrm((BATCH, D_MODEL), 1.0)
    offsets = jax.random.randint(next(ks), (BATCH, 1), 0, 2048, dtype=jnp.int32)
    positions = offsets + jnp.arange(SEQ, dtype=jnp.int32)[None, :]
    return {
        "x": x,
        "c": c,
        "positions": positions,
        "ada_w": nrm((DEPTH, 2, D_MODEL, 3 * D_MODEL), 0.5 * D_MODEL ** -0.5),
        "ada_b": nrm((DEPTH, 2, 3 * D_MODEL), 0.02),
        "norm_mix_g": gain((DEPTH, D_MODEL)),
        "norm_ffn_g": gain((DEPTH, D_MODEL)),
        "w_in": nrm((DEPTH, D_MODEL, IN_COLS), D_MODEL ** -0.5),
        "b_forget": jax.random.uniform(next(ks), (DEPTH, FOX_HEADS), jnp.float32, 1.0, 4.0),
        "q_norm_g": gain((DEPTH, MLA_Q_LORA)),
        "w_uq": nrm((DEPTH, MLA_Q_LORA, MLA_HEADS * MLA_QK_DIM), MLA_Q_LORA ** -0.5),
        "kv_norm_g": gain((DEPTH, MLA_KV_LORA)),
        "w_ukv": nrm((DEPTH, MLA_KV_LORA, MLA_HEADS * (MLA_NOPE_DIM + MLA_V_DIM)), MLA_KV_LORA ** -0.5),
        "w_branch_mla": nrm((DEPTH, MLA_WIDTH, D_MODEL), MLA_WIDTH ** -0.5),
        "w_branch_fox": nrm((DEPTH, FOX_WIDTH, D_MODEL), FOX_WIDTH ** -0.5),
        "w_out": nrm((DEPTH, D_MODEL, D_MODEL), D_MODEL ** -0.5),
        "dense_w_gate": nrm((N_DENSE, D_MODEL, D_FF), D_MODEL ** -0.5),
        "dense_w_up": nrm((N_DENSE, D_MODEL, D_FF), D_MODEL ** -0.5),
        "dense_w_down": nrm((N_DENSE, D_FF, D_MODEL), D_FF ** -0.5),
        "moe_w_router": nrm((N_MOE, D_MODEL, N_EXPERTS), D_MODEL ** -0.5),
        "moe_w_gate": nrm((N_MOE, N_EXPERTS, D_MODEL, D_FF), D_MODEL ** -0.5),
        "moe_w_up": nrm((N_MOE, N_EXPERTS, D_MODEL, D_FF), D_MODEL ** -0.5),
        "moe_w_down": nrm((N_MOE, N_EXPERTS, D_FF, D_MODEL), D_FF ** -0.5),
        "final_norm_g": gain((D_MODEL,)),
    }


def reference(x, c, positions, ada_w, ada_b, norm_mix_g, norm_ffn_g, w_in, b_forget,
              q_norm_g, w_uq, kv_norm_g, w_ukv, w_branch_mla, w_branch_fox, w_out,
              dense_w_gate, dense_w_up, dense_w_down,
              moe_w_router, moe_w_gate, moe_w_up, moe_w_down, final_norm_g):
    inv_freq = ROPE_THETA ** (-jnp.arange(0, MLA_ROPE_DIM, 2, dtype=jnp.float32) / MLA_ROPE_DIM)
    ang = positions.astype(jnp.float32)[..., None] * inv_freq
    cos = jnp.cos(ang).astype(x.dtype)
    sin = jnp.sin(ang).astype(x.dtype)

    for l in range(DEPTH):
        shift, scale, gate = adaln(c, ada_w[l, 0], ada_b[l, 0])
        h = rmsnorm(x, norm_mix_g[l]) * (1.0 + scale) + shift
        x = x + gate * hybrid_mixer(h, cos, sin, w_in[l], b_forget[l], q_norm_g[l], w_uq[l],
                                    kv_norm_g[l], w_ukv[l], w_branch_mla[l], w_branch_fox[l], w_out[l])
        shift, scale, gate = adaln(c, ada_w[l, 1], ada_b[l, 1])
        h = rmsnorm(x, norm_ffn_g[l]) * (1.0 + scale) + shift
        if l % 2 == 0:
            i = l // 2
            f = swiglu(h, dense_w_gate[i], dense_w_up[i], dense_w_down[i])
        else:
            i = l // 2
            f = moe_swiglu(h, moe_w_router[i], moe_w_gate[i], moe_w_up[i], moe_w_down[i])
        x = x + gate * f
    return rmsnorm(x, final_norm_g)
```

```python
import functools

import jax
import jax.numpy as jnp
import numpy as np
from jax import lax
from jax.experimental import pallas as pl
from jax.experimental.pallas import tpu as pltpu

D_MODEL = 1024
HEADS = 8
MLA_NOPE = 64
MLA_ROPE = 32
MLA_V = 64
MLA_Q_LORA = 256
MLA_KV_LORA = 128
ROPE_THETA = 10000.0
FOX_DIM = 64
D_FF = 3584
N_EXPERTS = 8
RMS_EPS = 1e-6
LANES = 128
HEAD_W = HEADS * LANES
MLA_SCALE = float((MLA_NOPE + MLA_ROPE) ** -0.5)
FOX_SCALE = 0.125
NEG_BIG = -1e30
VMEM_LIMIT = 56 * 1024 * 1024

IN_WIDTHS = (MLA_Q_LORA, MLA_KV_LORA, MLA_ROPE, 512, 512, 512, HEADS, D_MODEL, D_MODEL)
IN_OFFS = tuple(int(v) for v in np.cumsum((0,) + IN_WIDTHS))

A_CQ = (0, 256)
A_CKV = (256, 384)
A_T1 = (384, 512)
A_T2 = (512, 640)
A_Z = (640, 768)
A_FQ = (768, 1792)
A_FK = (1792, 2816)
A_FV = (2816, 3328)
A_COLS = 3328


def _cparams(n_axes):
    return pltpu.CompilerParams(dimension_semantics=("arbitrary",) * n_axes,
                                vmem_limit_bytes=VMEM_LIMIT)


def _dot(a, b):
    return jnp.dot(a, b, preferred_element_type=jnp.float32)


def _dot_nt(a, b):
    return lax.dot_general(a, b, (((1,), (1,)), ((), ())), preferred_element_type=jnp.float32)


def _bf(x):
    return x.astype(jnp.bfloat16)


def _split3(x):
    hi = _bf(x)
    r = x - hi.astype(jnp.float32)
    mid = _bf(r)
    lo = _bf(r - mid.astype(jnp.float32))
    return hi, mid, lo


def _rms(x, g):
    return x * lax.rsqrt(jnp.mean(x * x, axis=-1, keepdims=True) + RMS_EPS) * g


def _sigmoid(x):
    return 1.0 / (1.0 + jnp.exp(-x))


def _adaln_kernel(c_ref, w_ref, b_ref, o_ref):
    c = c_ref[...]
    a = c * _sigmoid(c)
    a_hi, a_mid, _ = _split3(a)
    w = w_ref[0]
    w_hi = _bf(w)
    w_lo = _bf(w - w_hi.astype(jnp.float32))
    o_ref[0] = _dot(a_hi, w_hi) + _dot(a_mid, w_hi) + _dot(a_hi, w_lo) + b_ref[0]


def _adaln(c, w, b):
    g, d, n = w.shape
    bsz = c.shape[0]
    tn = 1024
    return pl.pallas_call(
        _adaln_kernel,
        grid=(g, n // tn),
        in_specs=[pl.BlockSpec((bsz, d), lambda i, j: (0, 0)),
                  pl.BlockSpec((1, d, tn), lambda i, j: (i, 0, j)),
                  pl.BlockSpec((1, 1, tn), lambda i, j: (i, 0, j))],
        out_specs=pl.BlockSpec((1, bsz, tn), lambda i, j: (i, 0, j)),
        out_shape=jax.ShapeDtypeStruct((g, bsz, n), jnp.float32),
        compiler_params=_cparams(2),
        name="adaln",
    )(c, w, b)


def _rope_kernel(pos_ref, freq_ref, mask_ref, cos_ref, sin_ref):
    ang = pos_ref[0] * freq_ref[...]
    rope = mask_ref[0:1, :]
    nope = mask_ref[1:2, :]
    cos_ref[0] = jnp.cos(ang) * rope + nope
    sin_ref[0] = jnp.sin(ang) * rope


def _rope_tables(positions):
    bsz, seq = positions.shape
    inv = ROPE_THETA ** (-jnp.arange(0, MLA_ROPE, 2, dtype=jnp.float32) / MLA_ROPE)
    zeros = jnp.zeros((MLA_NOPE,), jnp.float32)
    freq = jnp.concatenate([zeros, inv, inv, jnp.zeros((32,), jnp.float32)])[None, :]
    lane = np.arange(LANES)
    mask = jnp.asarray(np.stack([(lane >= 64) & (lane < 96), lane < 64]).astype(np.float32))
    mask = jnp.concatenate([mask, jnp.zeros((6, LANES), jnp.float32)], axis=0)
    pos = positions.astype(jnp.float32)[..., None]
    ts = min(seq, 1024)
    return pl.pallas_call(
        _rope_kernel,
        grid=(bsz, seq // ts),
        in_specs=[pl.BlockSpec((1, ts, 1), lambda b, i: (b, i, 0)),
                  pl.BlockSpec((1, LANES), lambda b, i: (0, 0)),
                  pl.BlockSpec((8, LANES), lambda b, i: (0, 0))],
        out_specs=[pl.BlockSpec((1, ts, LANES), lambda b, i: (b, i, 0))] * 2,
        out_shape=[jax.ShapeDtypeStruct((bsz, seq, LANES), jnp.float32)] * 2,
        compiler_params=_cparams(2),
        name="rope_tables",
    )(pos, freq, mask)


def _mix_in_kernel(x_ref, shift_ref, scale_ref, g_ref, wa_ref, qg_ref, kvg_ref, w1_ref, w2_ref,
                   wk_ref, wv_ref, cos_ref, sin_ref, bf_ref, p_ref, ones_ref,
                   qm_ref, km_ref, vmt_ref, fq_ref, fk_ref, fvt_ref, carry_ref):
    tm = x_ref.shape[1]
    x = x_ref[0]
    h = _rms(x, g_ref[...]) * (1.0 + scale_ref[0]) + shift_ref[0]
    hb = _bf(h)

    def proj(cols):
        return _dot(hb, wa_ref[:, cols[0]:cols[1]])

    cos = cos_ref[0]
    sin = sin_ref[0]

    cqn = _bf(_rms(proj(A_CQ), qg_ref[...]))
    q1 = _dot(cqn, w1_ref[...])
    q2 = _dot(cqn, w2_ref[...])
    ckvn = _bf(_rms(proj(A_CKV), kvg_ref[...]))
    kp = _dot(ckvn, wk_ref[...])
    krope = proj(A_T1) * cos + proj(A_T2) * sin
    for hd in range(HEADS):
        sl = slice(hd * LANES, (hd + 1) * LANES)
        qm_ref[0, :, sl] = _bf((q1[:, sl] * cos + q2[:, sl] * sin) * MLA_SCALE)
        km_ref[0, :, sl] = _bf(kp[:, sl] + krope)
    vmt_ref[0] = _bf(_dot(ckvn, wv_ref[...]).T)

    @pl.when(pl.program_id(1) == 0)
    def _():
        carry_ref[...] = jnp.zeros_like(carry_ref)

    z = proj(A_Z) + bf_ref[...]
    lf = jnp.minimum(z, 0.0) - jnp.log(1.0 + jnp.exp(-jnp.abs(z)))
    row = lax.broadcasted_iota(jnp.int32, (tm, tm), 0)
    col = lax.broadcasted_iota(jnp.int32, (tm, tm), 1)
    tri = jnp.where(row >= col, 1.0, 0.0).astype(jnp.bfloat16)
    hi, mid, lo = _split3(lf)
    cum = _dot(tri, hi) + _dot(tri, mid) + _dot(tri, lo) + carry_ref[0:1, :]
    carry_ref[0:1, :] = cum[tm - 1:tm, :]
    chi, cmid, clo = _split3(cum)
    lane = lax.broadcasted_iota(jnp.int32, (tm, LANES), 1)
    pieces = jnp.where(lane < 8, chi, jnp.where(lane < 16, cmid, clo))
    extras = _dot(pieces, p_ref[...]) + ones_ref[...]
    fq_ref[0] = _bf(proj(A_FQ) + extras[:, :HEAD_W])
    fk_ref[0] = _bf(proj(A_FK) + extras[:, HEAD_W:])
    fvt_ref[0] = _bf(proj(A_FV).T)


def _mix_in(x, shift, scale, g, wa, qg, kvg, w1, w2, wk, wv, cos, sin, bfr, pmat, ones, tm):
    bsz, seq, d = x.shape
    full = lambda shape: pl.BlockSpec(shape, lambda b, i: (0,) * len(shape))
    tok = lambda w: pl.BlockSpec((1, tm, w), lambda b, i: (b, i, 0))
    per_b = pl.BlockSpec((1, 1, d), lambda b, i: (b, 0, 0))
    tr = pl.BlockSpec((1, 512, tm), lambda b, i: (b, 0, i))
    act = jax.ShapeDtypeStruct((bsz, seq, HEAD_W), jnp.bfloat16)
    act_t = jax.ShapeDtypeStruct((bsz, 512, seq), jnp.bfloat16)
    return pl.pallas_call(
        _mix_in_kernel,
        grid=(bsz, seq // tm),
        in_specs=[tok(d), per_b, per_b, full((1, d)), full(wa.shape), full(qg.shape), full(kvg.shape),
                  full(w1.shape), full(w2.shape), full(wk.shape), full(wv.shape),
                  tok(LANES), tok(LANES), full(bfr.shape), full(pmat.shape), full(ones.shape)],
        out_specs=[tok(HEAD_W), tok(HEAD_W), tr, tok(HEAD_W), tok(HEAD_W), tr],
        out_shape=[act, act, act_t, act, act, act_t],
        scratch_shapes=[pltpu.VMEM((8, LANES), jnp.float32)],
        compiler_params=_cparams(2),
        name="mix_in",
    )(x, shift, scale, g, wa, qg, kvg, w1, w2, wk, wv, cos, sin, bfr, pmat, ones)


def _attn_kernel(q_ref, k_ref, vt_ref, o_ref, m_ref, l_ref, acc_ref):
    qi = pl.program_id(2)
    ki = pl.program_id(3)
    tq = q_ref.shape[1]
    tk = k_ref.shape[1]
    dv = acc_ref.shape[1]

    @pl.when(ki == 0)
    def _():
        m_ref[...] = jnp.full_like(m_ref, NEG_BIG)
        l_ref[...] = jnp.zeros_like(l_ref)
        acc_ref[...] = jnp.zeros_like(acc_ref)

    def step(masked):
        for hh in range(2):
            sl = slice(hh * LANES, (hh + 1) * LANES)
            s = _dot_nt(k_ref[0, :, sl], q_ref[0, :, sl])
            if masked:
                kpos = lax.broadcasted_iota(jnp.int32, (tk, tq), 0)
                qpos = lax.broadcasted_iota(jnp.int32, (tk, tq), 1)
                s = jnp.where(kpos <= qpos, s, NEG_BIG)
            m_old = m_ref[hh, 0:1, :]
            m_new = jnp.maximum(m_old, jnp.max(s, axis=0, keepdims=True))
            alpha = jnp.exp(m_old - m_new)
            p = jnp.exp(s - m_new)
            l_ref[hh, 0:1, :] = alpha * l_ref[hh, 0:1, :] + jnp.sum(p, axis=0, keepdims=True)
            pv = _dot(vt_ref[0, hh * dv:(hh + 1) * dv, :], _bf(p))
            acc_ref[hh] = alpha * acc_ref[hh] + pv
            m_ref[hh, 0:1, :] = m_new

    @pl.when(ki < qi)
    def _():
        step(False)

    @pl.when(ki == qi)
    def _():
        step(True)
        o0 = acc_ref[0] / l_ref[0, 0:1, :]
        o1 = acc_ref[1] / l_ref[1, 0:1, :]
        o_ref[0] = _bf(jnp.concatenate([o0, o1], axis=0).T)


def _attention(q, k, vt, t):
    bsz, seq, _ = q.shape
    dv = vt.shape[1] // HEADS
    n = seq // t
    return pl.pallas_call(
        _attn_kernel,
        grid=(bsz, HEADS // 2, n, n),
        in_specs=[pl.BlockSpec((1, t, 2 * LANES), lambda b, j, qi, ki: (b, qi, j)),
                  pl.BlockSpec((1, t, 2 * LANES), lambda b, j, qi, ki: (b, jnp.minimum(ki, qi), j)),
                  pl.BlockSpec((1, 2 * dv, t), lambda b, j, qi, ki: (b, j, jnp.minimum(ki, qi)))],
        out_specs=pl.BlockSpec((1, t, 2 * dv), lambda b, j, qi, ki: (b, qi, j)),
        out_shape=jax.ShapeDtypeStruct((bsz, seq, HEADS * dv), jnp.bfloat16),
        scratch_shapes=[pltpu.VMEM((2, 8, t), jnp.float32),
                        pltpu.VMEM((2, 8, t), jnp.float32),
                        pltpu.VMEM((2, dv, t), jnp.float32)],
        compiler_params=_cparams(4),
        name="attention",
    )(q, k, vt)


def _mix_out_kernel(x_ref, ym_ref, yf_ref, shift1_ref, scale1_ref, gate1_ref, g1_ref,
                    shift2_ref, scale2_ref, g2_ref, wg_ref, wm_ref, wf_ref, wo_ref, wr_ref,
                    x1_ref, h2_ref, cw_ref, *, route):
    x = x_ref[0]
    hb = _bf(_rms(x, g1_ref[...]) * (1.0 + scale1_ref[0]) + shift1_ref[0])
    ga = _dot(hb, wg_ref[:, :D_MODEL])
    gb = _dot(hb, wg_ref[:, D_MODEL:])
    merged = _sigmoid(ga) * _dot(ym_ref[0], wm_ref[...]) + _sigmoid(gb) * _dot(yf_ref[0], wf_ref[...])
    x1 = x + gate1_ref[0] * _dot(_bf(merged), wo_ref[...])
    x1_ref[0] = x1
    h2 = _rms(x1, g2_ref[...]) * (1.0 + scale2_ref[0]) + shift2_ref[0]
    h2_ref[0] = _bf(h2)
    if route:
        h_hi, h_mid, _ = _split3(h2)
        lg = _dot(h_hi, wr_ref[0]) + _dot(h_mid, wr_ref[0]) + _dot(h_hi, wr_ref[1])
        lane = lax.broadcasted_iota(jnp.int32, lg.shape, 1).astype(jnp.float32)
        lg = jnp.where(lane < N_EXPERTS, lg, -jnp.inf)
        m1 = jnp.max(lg, axis=1, keepdims=True)
        i1 = jnp.min(jnp.where(lg == m1, lane, float(LANES)), axis=1, keepdims=True)
        lg2 = jnp.where(lane == i1, -jnp.inf, lg)
        m2 = jnp.max(lg2, axis=1, keepdims=True)
        i2 = jnp.min(jnp.where(lg2 == m2, lane, float(LANES)), axis=1, keepdims=True)
        e2 = jnp.exp(m2 - m1)
        w1 = 1.0 / (1.0 + e2)
        cw_ref[0] = jnp.where(lane == i1, w1, 0.0) + jnp.where(lane == i2, e2 * w1, 0.0)
    else:
        cw_ref[0] = jnp.ones_like(cw_ref[0])


def _mix_out(x, ym, yf, mod1, g1, mod2, g2, wg, wm, wf, wo, wr, route, tm):
    bsz, seq, d = x.shape
    full = lambda shape: pl.BlockSpec(shape, lambda b, i: (0,) * len(shape))
    tok = lambda w: pl.BlockSpec((1, tm, w), lambda b, i: (b, i, 0))
    per_b = pl.BlockSpec((1, 1, d), lambda b, i: (b, 0, 0))
    return pl.pallas_call(
        functools.partial(_mix_out_kernel, route=route),
        grid=(bsz, seq // tm),
        in_specs=[tok(d), tok(512), tok(512), per_b, per_b, per_b, full((1, d)),
                  per_b, per_b, full((1, d)), full(wg.shape), full(wm.shape), full(wf.shape),
                  full(wo.shape), full(wr.shape)],
        out_specs=[tok(d), tok(d), tok(LANES)],
        out_shape=[jax.ShapeDtypeStruct((bsz, seq, d), jnp.float32),
                   jax.ShapeDtypeStruct((bsz, seq, d), jnp.bfloat16),
                   jax.ShapeDtypeStruct((bsz, seq, LANES), jnp.float32)],
        compiler_params=_cparams(2),
        name="mix_out",
    )(x, ym, yf, mod1[0], mod1[1], mod1[2], g1, mod2[0], mod2[1], g2, wg, wm, wf, wo, wr)


def _ffn_kernel(x_ref, h_ref, cw_ref, gate_ref, gf_ref, wg_ref, wu_ref, wd_ref, o_ref, acc_ref, *, final):
    e = pl.program_id(2)
    j = pl.program_id(3)
    last = (e == pl.num_programs(2) - 1) & (j == pl.num_programs(3) - 1)

    @pl.when((e == 0) & (j == 0))
    def _():
        acc_ref[...] = jnp.zeros_like(acc_ref)

    h = h_ref[0]
    gp = _dot(h, wg_ref[0])
    up = _dot(h, wu_ref[0])
    lane = lax.broadcasted_iota(jnp.int32, cw_ref.shape[1:], 1)
    cwe = jnp.sum(jnp.where(lane == e, cw_ref[0], 0.0), axis=1, keepdims=True)
    act = gp * _sigmoid(gp) * up * cwe
    acc_ref[...] += _dot(_bf(act), wd_ref[0])

    @pl.when(last)
    def _():
        y = x_ref[0] + gate_ref[0] * acc_ref[...]
        if final:
            y = _rms(y, gf_ref[...])
        o_ref[0] = y


def _ffn(x, h, cw, gate, gf, wg, wu, wd, final, tm, tf):
    bsz, seq, d = x.shape
    ne, _, dff = wg.shape
    tok = lambda w: pl.BlockSpec((1, tm, w), lambda b, i, e, j: (b, i, 0))
    return pl.pallas_call(
        functools.partial(_ffn_kernel, final=final),
        grid=(bsz, seq // tm, ne, dff // tf),
        in_specs=[tok(d), tok(d), tok(LANES),
                  pl.BlockSpec((1, 1, d), lambda b, i, e, j: (b, 0, 0)),
                  pl.BlockSpec((1, d), lambda b, i, e, j: (0, 0)),
                  pl.BlockSpec((1, d, tf), lambda b, i, e, j: (e, 0, j)),
                  pl.BlockSpec((1, d, tf), lambda b, i, e, j: (e, 0, j)),
                  pl.BlockSpec((1, tf, d), lambda b, i, e, j: (e, j, 0))],
        out_specs=tok(d),
        out_shape=jax.ShapeDtypeStruct((bsz, seq, d), jnp.float32),
        scratch_shapes=[pltpu.VMEM((tm, d), jnp.float32)],
        compiler_params=_cparams(4),
        name="ffn",
    )(x, h, cw, gate, gf, wg, wu, wd)


def _pad_heads(w, widths, total=LANES):
    k = w.shape[0]
    per = sum(widths)
    w = w.reshape(k, HEADS, per)
    w = jnp.pad(w, ((0, 0), (0, 0), (0, total - per)))
    return w.reshape(k, HEADS * total)


def _rope_swap(w_rope):
    half = w_rope.shape[-1] // 2
    return jnp.concatenate([-w_rope[..., half:], w_rope[..., :half]], axis=-1)


def _layer_weights(w_in, w_uq, w_ukv):
    o = IN_OFFS
    seg = lambda i: w_in[:, o[i]:o[i + 1]]
    d = w_in.shape[0]
    zeros = lambda n: jnp.zeros((d, n), jnp.float32)
    w_kr = seg(2)
    t1 = jnp.concatenate([zeros(MLA_NOPE), w_kr, zeros(32)], axis=1)
    t2 = jnp.concatenate([zeros(MLA_NOPE), _rope_swap(w_kr), zeros(32)], axis=1)
    wz = jnp.concatenate([seg(6)] * 3 + [zeros(LANES - 3 * HEADS)], axis=1)
    wa = jnp.concatenate([seg(0), seg(1), t1, t2, wz,
                          _pad_heads(seg(3) * FOX_SCALE, (FOX_DIM,)), _pad_heads(seg(4), (FOX_DIM,)),
                          seg(5)], axis=1)
    wg = jnp.concatenate([seg(7), seg(8)], axis=1)
    uq = w_uq.reshape(MLA_Q_LORA, HEADS, MLA_NOPE + MLA_ROPE)
    nope, rope = uq[..., :MLA_NOPE], uq[..., MLA_NOPE:]
    pad = jnp.zeros((MLA_Q_LORA, HEADS, LANES - MLA_NOPE - MLA_ROPE), jnp.float32)
    w1 = jnp.concatenate([nope, rope, pad], axis=-1).reshape(MLA_Q_LORA, HEAD_W)
    w2 = jnp.concatenate([jnp.zeros_like(nope), _rope_swap(rope), pad], axis=-1).reshape(MLA_Q_LORA, HEAD_W)
    ukv = w_ukv.reshape(MLA_KV_LORA, HEADS, MLA_NOPE + MLA_V)
    wk = _pad_heads(ukv[..., :MLA_NOPE].reshape(MLA_KV_LORA, -1), (MLA_NOPE,))
    wv = ukv[..., MLA_NOPE:].reshape(MLA_KV_LORA, HEADS * MLA_V)
    return tuple(_bf(t) for t in (wa, wg, w1, w2, wk, wv))


def _fox_extras():
    p = np.zeros((LANES, 2 * HEAD_W), np.float32)
    ones = np.zeros((1, 2 * HEAD_W), np.float32)
    for hd in range(HEADS):
        base = hd * LANES + FOX_DIM
        for piece in range(3):
            p[piece * HEADS + hd, base + piece] = 1.0
            ones[0, base + 3 + piece] = 1.0
            ones[0, HEAD_W + base + piece] = 1.0
            p[piece * HEADS + hd, HEAD_W + base + 3 + piece] = -1.0
    return jnp.asarray(p, jnp.bfloat16), jnp.asarray(ones, jnp.float32)


def kernel(x, c, positions, ada_w, ada_b, norm_mix_g, norm_ffn_g, w_in, b_forget, q_norm_g, w_uq,
           kv_norm_g, w_ukv, w_branch_mla, w_branch_fox, w_out, dense_w_gate, dense_w_up, dense_w_down,
           moe_w_router, moe_w_gate, moe_w_up, moe_w_down, final_norm_g):
    bsz, seq, d = x.shape
    depth = w_in.shape[0]
    tm = min(512, seq)
    t_attn = min(512, seq)
    tm_ffn = min(1024, seq)
    tf = 512

    mod = _adaln(c, ada_w.reshape(depth * 2, d, 3 * d), ada_b.reshape(depth * 2, 1, 3 * d))
    mod = mod.reshape(depth, 2, bsz, 1, 3, d)
    cos, sin = _rope_tables(positions)
    pmat, ones = _fox_extras()
    row = lambda v: v.reshape(1, -1)

    for l in range(depth):
        wa, wg, w1, w2, wk, wv = _layer_weights(w_in[l], w_uq[l], w_ukv[l])
        mod1 = [mod[l, 0, :, :, i] for i in range(3)]
        mod2 = [mod[l, 1, :, :, i] for i in range(3)]
        bfr = jnp.concatenate([b_forget[l]] * 3 + [jnp.zeros((LANES - 3 * HEADS,), jnp.float32)])[None, :]
        qm, km, vmt, fq, fk, fvt = _mix_in(
            x, mod1[0], mod1[1], row(norm_mix_g[l]), wa, row(q_norm_g[l]), row(kv_norm_g[l]),
            w1, w2, wk, wv, cos, sin, bfr, pmat, ones, tm)
        y_mla = _attention(qm, km, vmt, t_attn)
        y_fox = _attention(fq, fk, fvt, t_attn)
        moe = l % 2 == 1
        if moe:
            wr = jnp.pad(moe_w_router[l // 2], ((0, 0), (0, LANES - N_EXPERTS)))
            wr_hi = _bf(wr)
            wr = jnp.stack([wr_hi, _bf(wr - wr_hi.astype(jnp.float32))])
            fg, fu, fd = moe_w_gate[l // 2], moe_w_up[l // 2], moe_w_down[l // 2]
        else:
            wr = jnp.zeros((2, d, LANES), jnp.bfloat16)
            fg, fu, fd = dense_w_gate[l // 2][None], dense_w_up[l // 2][None], dense_w_down[l // 2][None]
        x1, h2, cw = _mix_out(x, y_mla, y_fox, mod1, row(norm_mix_g[l]), mod2, row(norm_ffn_g[l]),
                              wg, _bf(w_branch_mla[l]), _bf(w_branch_fox[l]), _bf(w_out[l]), wr, moe, tm)
        x = _ffn(x1, h2, cw, mod2[2], row(final_norm_g), _bf(fg), _bf(fu), _bf(fd),
                 l == depth - 1, tm_ffn, tf)
    return x
```

```python
import functools

import jax
import jax.numpy as jnp
import numpy as np
from jax import lax
from jax.experimental import pallas as pl
from jax.experimental.pallas import tpu as pltpu

D_MODEL = 1024
HEADS = 8
MLA_NOPE = 64
MLA_ROPE = 32
MLA_V = 64
MLA_Q_LORA = 256
MLA_KV_LORA = 128
ROPE_THETA = 10000.0
FOX_DIM = 64
D_FF = 3584
N_EXPERTS = 8
RMS_EPS = 1e-6
LANES = 128
HEAD_W = HEADS * LANES
LOG2E = 1.4426950408889634
MLA_SCALE = float((MLA_NOPE + MLA_ROPE) ** -0.5)
FOX_SCALE = 0.125
NEG_BIG = -1e30
RC = 32
VMEM_LIMIT = 56 * 1024 * 1024

IN_WIDTHS = (MLA_Q_LORA, MLA_KV_LORA, MLA_ROPE, 512, 512, 512, HEADS, D_MODEL, D_MODEL)
IN_OFFS = tuple(int(v) for v in np.cumsum((0,) + IN_WIDTHS))

A_CQ = (0, 256)
A_CKV = (256, 384)
A_T1 = (384, 512)
A_T2 = (512, 640)
A_Z = (640, 768)
A_FQ = (768, 1792)
A_FK = (1792, 2816)
A_FV = (2816, 3328)
A_COLS = 3328


def _cparams(n_axes):
    return pltpu.CompilerParams(dimension_semantics=("arbitrary",) * n_axes,
                                vmem_limit_bytes=VMEM_LIMIT)


def _dot(a, b):
    return jnp.dot(a, b, preferred_element_type=jnp.float32)


def _dot_nt(a, b):
    return lax.dot_general(a, b, (((1,), (1,)), ((), ())), preferred_element_type=jnp.float32)


def _bf(x):
    return x.astype(jnp.bfloat16)


def _split3(x):
    hi = _bf(x)
    r = x - hi.astype(jnp.float32)
    mid = _bf(r)
    lo = _bf(r - mid.astype(jnp.float32))
    return hi, mid, lo


def _rms(x, g):
    return x * lax.rsqrt(jnp.mean(x * x, axis=-1, keepdims=True) + RMS_EPS) * g


def _sigmoid(x):
    return 1.0 / (1.0 + jnp.exp(-x))


def _adaln_kernel(c_ref, w_ref, b_ref, o_ref):
    c = c_ref[...]
    a = c * _sigmoid(c)
    a_hi, a_mid, _ = _split3(a)
    w = w_ref[0]
    w_hi = _bf(w)
    w_lo = _bf(w - w_hi.astype(jnp.float32))
    o_ref[0] = _dot(a_hi, w_hi) + _dot(a_mid, w_hi) + _dot(a_hi, w_lo) + b_ref[0]


def _adaln(c, w, b):
    g, d, n = w.shape
    bsz = c.shape[0]
    tn = 1024
    return pl.pallas_call(
        _adaln_kernel,
        grid=(g, n // tn),
        in_specs=[pl.BlockSpec((bsz, d), lambda i, j: (0, 0)),
                  pl.BlockSpec((1, d, tn), lambda i, j: (i, 0, j)),
                  pl.BlockSpec((1, 1, tn), lambda i, j: (i, 0, j))],
        out_specs=pl.BlockSpec((1, bsz, tn), lambda i, j: (i, 0, j)),
        out_shape=jax.ShapeDtypeStruct((g, bsz, n), jnp.float32),
        compiler_params=_cparams(2),
        name="adaln",
    )(c, w, b)


def _rope_kernel(pos_ref, freq_ref, mask_ref, cos_ref, sin_ref):
    ang = pos_ref[0] * freq_ref[...]
    rope = mask_ref[0:1, :]
    nope = mask_ref[1:2, :]
    cos_ref[0] = jnp.cos(ang) * rope + nope
    sin_ref[0] = jnp.sin(ang) * rope


def _rope_tables(positions):
    bsz, seq = positions.shape
    inv = ROPE_THETA ** (-jnp.arange(0, MLA_ROPE, 2, dtype=jnp.float32) / MLA_ROPE)
    zeros = jnp.zeros((MLA_NOPE,), jnp.float32)
    freq = jnp.concatenate([zeros, inv, inv, jnp.zeros((32,), jnp.float32)])[None, :]
    lane = np.arange(LANES)
    mask = jnp.asarray(np.stack([(lane >= 64) & (lane < 96), lane < 64]).astype(np.float32))
    mask = jnp.concatenate([mask, jnp.zeros((6, LANES), jnp.float32)], axis=0)
    pos = positions.astype(jnp.float32)[..., None]
    ts = min(seq, 1024)
    return pl.pallas_call(
        _rope_kernel,
        grid=(bsz, seq // ts),
        in_specs=[pl.BlockSpec((1, ts, 1), lambda b, i: (b, i, 0)),
                  pl.BlockSpec((1, LANES), lambda b, i: (0, 0)),
                  pl.BlockSpec((8, LANES), lambda b, i: (0, 0))],
        out_specs=[pl.BlockSpec((1, ts, LANES), lambda b, i: (b, i, 0))] * 2,
        out_shape=[jax.ShapeDtypeStruct((bsz, seq, LANES), jnp.float32)] * 2,
        compiler_params=_cparams(2),
        name="rope_tables",
    )(pos, freq, mask)


def _mix_in_kernel(x_ref, shift_ref, scale_ref, g_ref, wa_ref, qg_ref, kvg_ref, w1_ref, w2_ref,
                   wk_ref, wv_ref, cos_ref, sin_ref, bf_ref, p_ref, ones_ref,
                   qm_ref, km_ref, vmt_ref, fq_ref, fk_ref, fvt_ref, carry_ref):
    tm = x_ref.shape[1]
    x = x_ref[0]
    h = _rms(x, g_ref[...]) * (1.0 + scale_ref[0]) + shift_ref[0]
    hb = _bf(h)

    def proj(cols):
        return _dot(hb, wa_ref[:, cols[0]:cols[1]])

    cos = cos_ref[0]
    sin = sin_ref[0]

    cqn = _bf(_rms(proj(A_CQ), qg_ref[...]))
    q1 = _dot(cqn, w1_ref[...])
    q2 = _dot(cqn, w2_ref[...])
    ckvn = _bf(_rms(proj(A_CKV), kvg_ref[...]))
    kp = _dot(ckvn, wk_ref[...])
    krope = proj(A_T1) * cos + proj(A_T2) * sin
    for hd in range(HEADS):
        sl = slice(hd * LANES, (hd + 1) * LANES)
        qm_ref[0, :, sl] = _bf((q1[:, sl] * cos + q2[:, sl] * sin) * (MLA_SCALE * LOG2E))
        km_ref[0, :, sl] = _bf(kp[:, sl] + krope)
    vmt_ref[0, 0] = _bf(_dot(ckvn, wv_ref[...]).T)

    @pl.when(pl.program_id(1) == 0)
    def _():
        carry_ref[...] = jnp.zeros_like(carry_ref)

    z = proj(A_Z) + bf_ref[...]
    lf = jnp.minimum(z, 0.0) - jnp.log(1.0 + jnp.exp(-jnp.abs(z)))
    row = lax.broadcasted_iota(jnp.int32, (tm, tm), 0)
    col = lax.broadcasted_iota(jnp.int32, (tm, tm), 1)
    tri = jnp.where(row >= col, 1.0, 0.0).astype(jnp.bfloat16)
    hi, mid, lo = _split3(lf)
    cum = _dot(tri, hi) + _dot(tri, mid) + _dot(tri, lo) + carry_ref[0:1, :]
    carry_ref[0:1, :] = cum[tm - 1:tm, :]
    chi, cmid, clo = _split3(cum * LOG2E)
    lane = lax.broadcasted_iota(jnp.int32, (tm, LANES), 1)
    pieces = jnp.where(lane < 8, chi, jnp.where(lane < 16, cmid, clo))
    extras = _dot(pieces, p_ref[...]) + ones_ref[...]
    fq_ref[0] = _bf(proj(A_FQ) * LOG2E + extras[:, :HEAD_W])
    fk_ref[0] = _bf(proj(A_FK) + extras[:, HEAD_W:])
    fvt_ref[0, 0] = _bf(proj(A_FV).T)


def _mix_in(x, shift, scale, g, wa, qg, kvg, w1, w2, wk, wv, cos, sin, bfr, pmat, ones, tm):
    bsz, seq, d = x.shape
    full = lambda shape: pl.BlockSpec(shape, lambda b, i: (0,) * len(shape))
    tok = lambda w: pl.BlockSpec((1, tm, w), lambda b, i: (b, i, 0))
    per_b = pl.BlockSpec((1, 1, d), lambda b, i: (b, 0, 0))
    tr = pl.BlockSpec((1, 1, 512, tm), lambda b, i: (b, i, 0, 0))
    act = jax.ShapeDtypeStruct((bsz, seq, HEAD_W), jnp.bfloat16)
    act_t = jax.ShapeDtypeStruct((bsz, seq // tm, 512, tm), jnp.bfloat16)
    return pl.pallas_call(
        _mix_in_kernel,
        grid=(bsz, seq // tm),
        in_specs=[tok(d), per_b, per_b, full((1, d)), full(wa.shape), full(qg.shape), full(kvg.shape),
                  full(w1.shape), full(w2.shape), full(wk.shape), full(wv.shape),
                  tok(LANES), tok(LANES), full(bfr.shape), full(pmat.shape), full(ones.shape)],
        out_specs=[tok(HEAD_W), tok(HEAD_W), tr, tok(HEAD_W), tok(HEAD_W), tr],
        out_shape=[act, act, act_t, act, act, act_t],
        scratch_shapes=[pltpu.VMEM((8, LANES), jnp.float32)],
        compiler_params=_cparams(2),
        name="mix_in",
    )(x, shift, scale, g, wa, qg, kvg, w1, w2, wk, wv, cos, sin, bfr, pmat, ones)


def _attn_kernel(q_ref, k_ref, vt_ref, o_ref, m_ref, acc_ref, s_ref, p_ref):
    qi = pl.program_id(2)
    tq = q_ref.shape[1]
    tk = vt_ref.shape[3]
    dv = acc_ref.shape[1] - 16

    m_ref[...] = jnp.full_like(m_ref, NEG_BIG)
    acc_ref[...] = jnp.zeros_like(acc_ref)

    def block(ki, masked):
        rows_k = pl.ds(pl.multiple_of(ki * tk, tk), tk)
        for hh in range(2):
            sl = slice(hh * LANES, (hh + 1) * LANES)
            s = _dot_nt(k_ref[0, rows_k, sl], q_ref[0, :, sl])
            if masked:
                kpos = lax.broadcasted_iota(jnp.int32, (tk, tq), 0)
                qpos = lax.broadcasted_iota(jnp.int32, (tk, tq), 1)
                s = jnp.where(kpos <= qpos, s, NEG_BIG)
            s_ref[hh] = s
        for hh in range(2):
            m_old = m_ref[hh, 0:1, :]
            m_new = jnp.maximum(m_old, jnp.max(s_ref[hh], axis=0, keepdims=True))
            alpha = jnp.exp2(m_old - m_new)
            m_ref[hh, 0:1, :] = m_new
            mb = jnp.broadcast_to(m_new, (RC, tq))
            for c in range(tk // RC):
                rows = slice(c * RC, (c + 1) * RC)
                p_ref[hh, rows, :] = _bf(jnp.exp2(s_ref[hh, rows, :] - mb))
            va = jnp.concatenate([vt_ref[0, ki, hh * dv:(hh + 1) * dv, :], jnp.ones((16, tk), jnp.bfloat16)], axis=0)
            acc_ref[hh] = alpha * acc_ref[hh] + _dot(va, p_ref[hh])

    def body(ki, carry):
        block(ki, False)
        return carry

    lax.fori_loop(0, qi, body, 0)
    block(qi, True)
    o0 = acc_ref[0, :dv, :] / acc_ref[0, dv:dv + 1, :]
    o1 = acc_ref[1, :dv, :] / acc_ref[1, dv:dv + 1, :]
    o_ref[0] = _bf(jnp.concatenate([o0, o1], axis=0).T)


def _attention(q, k, vt, t):
    bsz, seq, _ = q.shape
    dv = vt.shape[2] // HEADS
    n = seq // t
    return pl.pallas_call(
        _attn_kernel,
        grid=(bsz, HEADS // 2, n),
        in_specs=[pl.BlockSpec((1, t, 2 * LANES), lambda b, j, qi: (b, qi, j)),
                  pl.BlockSpec((1, seq, 2 * LANES), lambda b, j, qi: (b, 0, j)),
                  pl.BlockSpec((1, n, 2 * dv, t), lambda b, j, qi: (b, 0, j, 0))],
        out_specs=pl.BlockSpec((1, t, 2 * dv), lambda b, j, qi: (b, qi, j)),
        out_shape=jax.ShapeDtypeStruct((bsz, seq, HEADS * dv), jnp.bfloat16),
        scratch_shapes=[pltpu.VMEM((2, 8, t), jnp.float32),
                        pltpu.VMEM((2, dv + 16, t), jnp.float32),
                        pltpu.VMEM((2, t, t), jnp.float32),
                        pltpu.VMEM((2, t, t), jnp.bfloat16)],
        compiler_params=_cparams(3),
        name="attention",
    )(q, k, vt)


R_MASK = 8
R_I1 = 16
R_W1 = 18


def _mix_out_kernel(x_ref, ym_ref, yf_ref, shift1_ref, scale1_ref, gate1_ref, g1_ref,
                    shift2_ref, scale2_ref, g2_ref, wg_ref, wm_ref, wf_ref, wo_ref, wr_ref,
                    x1_ref, h2_ref, info_ref, cnt_ref, *, route):
    x = x_ref[0]
    hb = _bf(_rms(x, g1_ref[...]) * (1.0 + scale1_ref[0]) + shift1_ref[0])
    ga = _dot(hb, wg_ref[:, :D_MODEL])
    gb = _dot(hb, wg_ref[:, D_MODEL:])
    merged = _sigmoid(ga) * _dot(ym_ref[0], wm_ref[...]) + _sigmoid(gb) * _dot(yf_ref[0], wf_ref[...])
    x1 = x + gate1_ref[0] * _dot(_bf(merged), wo_ref[...])
    x1_ref[0] = x1
    h2 = _rms(x1, g2_ref[...]) * (1.0 + scale2_ref[0]) + shift2_ref[0]
    h2_ref[0] = h2.astype(h2_ref.dtype)

    @pl.when((pl.program_id(0) == 0) & (pl.program_id(1) == 0))
    def _():
        cnt_ref[...] = jnp.zeros_like(cnt_ref)

    if route:
        h_hi, h_mid, _ = _split3(h2)
        lg = _dot(h_hi, wr_ref[0]) + _dot(h_mid, wr_ref[0]) + _dot(h_hi, wr_ref[1])
        lane_i = lax.broadcasted_iota(jnp.int32, lg.shape, 1)
        lane = lane_i.astype(jnp.float32)
        lg = jnp.where(lane_i < N_EXPERTS, lg, -jnp.inf)
        m1 = jnp.max(lg, axis=1, keepdims=True)
        i1 = jnp.min(jnp.where(lg == m1, lane, float(LANES)), axis=1, keepdims=True)
        lg2 = jnp.where(lane == i1, -jnp.inf, lg)
        m2 = jnp.max(lg2, axis=1, keepdims=True)
        i2 = jnp.min(jnp.where(lg2 == m2, lane, float(LANES)), axis=1, keepdims=True)
        e2 = jnp.exp(m2 - m1)
        w1 = 1.0 / (1.0 + e2)
        mask = jnp.where(lane - R_MASK == i1, 1.0, jnp.where(lane - R_MASK == i2, 1.0, 0.0))
        info = mask + jnp.where(lane_i == R_I1, i1, 0.0) + jnp.where(lane_i == R_I1 + 1, i2, 0.0)
        info = info + jnp.where(lane_i == R_W1, w1, 0.0) + jnp.where(lane_i == R_W1 + 1, e2 * w1, 0.0)
        info_ref[0] = info
        cnt_ref[0:1, :] += jnp.sum(mask, axis=0, keepdims=True)
    else:
        info_ref[0] = jnp.zeros_like(info_ref[0])


def _mix_out(x, ym, yf, mod1, g1, mod2, g2, wg, wm, wf, wo, wr, route, tm):
    bsz, seq, d = x.shape
    full = lambda shape: pl.BlockSpec(shape, lambda b, i: (0,) * len(shape))
    tok = lambda w: pl.BlockSpec((1, tm, w), lambda b, i: (b, i, 0))
    per_b = pl.BlockSpec((1, 1, d), lambda b, i: (b, 0, 0))
    return pl.pallas_call(
        functools.partial(_mix_out_kernel, route=route),
        grid=(bsz, seq // tm),
        in_specs=[tok(d), tok(512), tok(512), per_b, per_b, per_b, full((1, d)),
                  per_b, per_b, full((1, d)), full(wg.shape), full(wm.shape), full(wf.shape),
                  full(wo.shape), full(wr.shape)],
        out_specs=[tok(d), tok(d), tok(LANES), full((8, LANES))],
        out_shape=[jax.ShapeDtypeStruct((bsz, seq, d), jnp.float32),
                   jax.ShapeDtypeStruct((bsz, seq, d), jnp.float32 if route else jnp.bfloat16),
                   jax.ShapeDtypeStruct((bsz, seq, LANES), jnp.float32),
                   jax.ShapeDtypeStruct((8, LANES), jnp.float32)],
        compiler_params=_cparams(2),
        name="mix_out",
    )(x, ym, yf, mod1[0], mod1[1], mod1[2], g1, mod2[0], mod2[1], g2, wg, wm, wf, wo, wr)


def _swiglu_chunk(h, wg_ref, wu_ref, wd_ref):
    gp = _dot(h, wg_ref[0])
    up = _dot(h, wu_ref[0])
    return _dot(_bf(gp * _sigmoid(gp) * up), wd_ref[0])


def _ffn_kernel(x_ref, h_ref, gate_ref, wg_ref, wu_ref, wd_ref, o_ref, acc_ref):
    j = pl.program_id(2)

    @pl.when(j == 0)
    def _():
        acc_ref[...] = jnp.zeros_like(acc_ref)

    acc_ref[...] += _swiglu_chunk(h_ref[0], wg_ref, wu_ref, wd_ref)

    @pl.when(j == pl.num_programs(2) - 1)
    def _():
        o_ref[0] = x_ref[0] + gate_ref[0] * acc_ref[...]


def _ffn(x, h, gate, wg, wu, wd, tm, tf):
    bsz, seq, d = x.shape
    dff = wg.shape[2]
    tok = pl.BlockSpec((1, tm, d), lambda b, i, j: (b, i, 0))
    return pl.pallas_call(
        _ffn_kernel,
        grid=(bsz, seq // tm, dff // tf),
        in_specs=[tok, tok, pl.BlockSpec((1, 1, d), lambda b, i, j: (b, 0, 0)),
                  pl.BlockSpec((1, d, tf), lambda b, i, j: (0, 0, j)),
                  pl.BlockSpec((1, d, tf), lambda b, i, j: (0, 0, j)),
                  pl.BlockSpec((1, tf, d), lambda b, i, j: (0, j, 0))],
        out_specs=tok,
        out_shape=jax.ShapeDtypeStruct((bsz, seq, d), jnp.float32),
        scratch_shapes=[pltpu.VMEM((tm, d), jnp.float32)],
        compiler_params=_cparams(3),
        name="ffn",
    )(x, h, gate, wg, wu, wd)


def _route_pos_kernel(info_ref, offs_ref, pos_ref, carry_ref):
    tm = info_ref.shape[1]

    @pl.when((pl.program_id(0) == 0) & (pl.program_id(1) == 0))
    def _():
        carry_ref[...] = jnp.zeros_like(carry_ref)

    info = info_ref[0]
    lane_i = lax.broadcasted_iota(jnp.int32, info.shape, 1)
    mask = jnp.where(lane_i >= R_MASK, jnp.where(lane_i < R_MASK + N_EXPERTS, info, 0.0), 0.0)
    row = lax.broadcasted_iota(jnp.int32, (tm, tm), 0)
    col = lax.broadcasted_iota(jnp.int32, (tm, tm), 1)
    tri = jnp.where(row > col, 1.0, 0.0).astype(jnp.bfloat16)
    posv = _dot(tri, _bf(mask)) + carry_ref[0:1, :] + offs_ref[...]
    carry_ref[0:1, :] += jnp.sum(mask, axis=0, keepdims=True)
    lane = lane_i.astype(jnp.float32) - R_MASK
    pa = jnp.sum(jnp.where(lane == info[:, R_I1:R_I1 + 1], posv, 0.0), axis=1, keepdims=True)
    pb = jnp.sum(jnp.where(lane == info[:, R_I1 + 1:R_I1 + 2], posv, 0.0), axis=1, keepdims=True)
    both = jnp.where(lane_i == 0, pa, jnp.where(lane_i == 1, pb, 0.0))
    pos_ref[0] = both.T[0:8, :].astype(jnp.int32)


def _route_pos(info, offs, tm):
    bsz, seq, _ = info.shape
    nt = seq // tm
    return pl.pallas_call(
        _route_pos_kernel,
        grid=(bsz, nt),
        in_specs=[pl.BlockSpec((1, tm, LANES), lambda b, i: (b, i, 0)),
                  pl.BlockSpec((1, LANES), lambda b, i: (0, 0))],
        out_specs=pl.BlockSpec((1, 8, tm), lambda b, i: (b * nt + i, 0, 0)),
        out_shape=jax.ShapeDtypeStruct((bsz * nt, 8, tm), jnp.int32),
        scratch_shapes=[pltpu.VMEM((8, LANES), jnp.float32)],
        compiler_params=_cparams(2),
        name="route_pos",
    )(info, offs)


def _row_copy(src, s_row, dst, d_row, sem):
    return pltpu.make_async_copy(src.at[pl.ds(s_row, 1)], dst.at[pl.ds(d_row, 1)], sem)


def _dispatch_kernel(pos_ref, h_ref, zeros_ref, xs_ref, sem):
    del zeros_ref
    tm = pos_ref.shape[2]
    base = pl.program_id(0) * tm

    def issue(r, c):
        _row_copy(h_ref, base + r, xs_ref, pos_ref[0, 0, r], sem).start()
        _row_copy(h_ref, base + r, xs_ref, pos_ref[0, 1, r], sem).start()
        return c

    def drain(r, c):
        _row_copy(h_ref, base + r, xs_ref, pos_ref[0, 0, r], sem).wait()
        _row_copy(h_ref, base + r, xs_ref, pos_ref[0, 1, r], sem).wait()
        return c

    lax.fori_loop(0, tm, issue, 0)
    lax.fori_loop(0, tm, drain, 0)


def _dispatch(pos, h, rows):
    n, _, tm = pos.shape
    d = h.shape[1]
    return pl.pallas_call(
        _dispatch_kernel,
        grid=(n,),
        in_specs=[pl.BlockSpec((1, 8, tm), lambda i: (i, 0, 0), memory_space=pltpu.SMEM),
                  pl.BlockSpec(memory_space=pl.ANY),
                  pl.BlockSpec(memory_space=pl.ANY)],
        out_specs=pl.BlockSpec(memory_space=pl.ANY),
        out_shape=jax.ShapeDtypeStruct((rows, d), h.dtype),
        scratch_shapes=[pltpu.SemaphoreType.DMA(())],
        input_output_aliases={2: 0},
        compiler_params=_cparams(1),
        name="dispatch",
    )(pos, h, jnp.zeros((rows, d), h.dtype))


def _expert_ffn_kernel(te_ref, na_ref, xs_ref, wg_ref, wu_ref, wd_ref, ys_ref, acc_ref):
    del te_ref
    i = pl.program_id(0)
    j = pl.program_id(1)

    @pl.when(j == 0)
    def _():
        acc_ref[...] = jnp.zeros_like(acc_ref)

    @pl.when(i < na_ref[0])
    def _():
        acc_ref[...] += _swiglu_chunk(_bf(xs_ref[...]), wg_ref, wu_ref, wd_ref)

    @pl.when(j == pl.num_programs(1) - 1)
    def _():
        ys_ref[...] = acc_ref[...]


def _expert_ffn(tile_expert, n_active, xs, wg, wu, wd, tm, tf):
    rows, d = xs.shape
    dff = wg.shape[2]
    grid_spec = pltpu.PrefetchScalarGridSpec(
        num_scalar_prefetch=2,
        grid=(rows // tm, dff // tf),
        in_specs=[pl.BlockSpec((tm, d), lambda i, j, te, na: (i, 0)),
                  pl.BlockSpec((1, d, tf), lambda i, j, te, na: (te[i], 0, j)),
                  pl.BlockSpec((1, d, tf), lambda i, j, te, na: (te[i], 0, j)),
                  pl.BlockSpec((1, tf, d), lambda i, j, te, na: (te[i], j, 0))],
        out_specs=pl.BlockSpec((tm, d), lambda i, j, te, na: (i, 0)),
        scratch_shapes=[pltpu.VMEM((tm, d), jnp.float32)])
    return pl.pallas_call(
        _expert_ffn_kernel,
        grid_spec=grid_spec,
        out_shape=jax.ShapeDtypeStruct((rows, d), jnp.float32),
        compiler_params=_cparams(2),
        name="expert_ffn",
    )(tile_expert, n_active, xs, wg, wu, wd)


def _combine_kernel(pos_ref, ys_ref, x_ref, info_ref, gate_ref, gf_ref, o_ref, buf_a, buf_b, sem, *, final):
    tm = x_ref.shape[1]

    def issue(r, c):
        _row_copy(ys_ref, pos_ref[0, 0, r], buf_a, r, sem.at[0]).start()
        _row_copy(ys_ref, pos_ref[0, 1, r], buf_b, r, sem.at[1]).start()
        return c

    def drain(r, c):
        _row_copy(ys_ref, pos_ref[0, 0, r], buf_a, r, sem.at[0]).wait()
        _row_copy(ys_ref, pos_ref[0, 1, r], buf_b, r, sem.at[1]).wait()
        return c

    lax.fori_loop(0, tm, issue, 0)
    lax.fori_loop(0, tm, drain, 0)
    info = info_ref[0]
    f = info[:, R_W1:R_W1 + 1] * buf_a[...] + info[:, R_W1 + 1:R_W1 + 2] * buf_b[...]
    y = x_ref[0] + gate_ref[0] * f
    if final:
        y = _rms(y, gf_ref[...])
    o_ref[0] = y


def _combine(pos, ys, x, info, gate, gf, final):
    bsz, seq, d = x.shape
    tm = pos.shape[2]
    nt = seq // tm
    tok = lambda w: pl.BlockSpec((1, tm, w), lambda b, i: (b, i, 0))
    return pl.pallas_call(
        functools.partial(_combine_kernel, final=final),
        grid=(bsz, nt),
        in_specs=[pl.BlockSpec((1, 8, tm), lambda b, i: (b * nt + i, 0, 0), memory_space=pltpu.SMEM),
                  pl.BlockSpec(memory_space=pl.ANY),
                  tok(d), tok(LANES),
                  pl.BlockSpec((1, 1, d), lambda b, i: (b, 0, 0)),
                  pl.BlockSpec((1, d), lambda b, i: (0, 0))],
        out_specs=tok(d),
        out_shape=jax.ShapeDtypeStruct((bsz, seq, d), jnp.float32),
        scratch_shapes=[pltpu.VMEM((tm, d), jnp.float32), pltpu.VMEM((tm, d), jnp.float32),
                        pltpu.SemaphoreType.DMA((2,))],
        compiler_params=_cparams(2),
        name="combine",
    )(pos, ys, x, info, gate, gf)


def _moe(x1, h2, info, counts, gate, gf, wg, wu, wd, final, tm_tok, tm_rows, tf):
    bsz, seq, d = x1.shape
    n_tok = bsz * seq
    n_tiles = 2 * n_tok // tm_rows + N_EXPERTS
    cnt = counts[0, R_MASK:R_MASK + N_EXPERTS].astype(jnp.int32)
    tiles = (cnt + tm_rows - 1) // tm_rows
    ends = jnp.cumsum(tiles)
    starts = ((ends - tiles) * tm_rows).astype(jnp.float32)
    offs = jnp.zeros((1, LANES), jnp.float32).at[0, R_MASK:R_MASK + N_EXPERTS].set(starts)
    n_active = ends[-1:]
    tile_ids = jnp.minimum(jnp.arange(n_tiles, dtype=jnp.int32), n_active - 1)
    tile_expert = jnp.sum(ends[None, :] <= tile_ids[:, None], axis=1).astype(jnp.int32)
    pos = _route_pos(info, offs, tm_tok)
    xs = _dispatch(pos, h2.reshape(n_tok, d), n_tiles * tm_rows)
    ys = _expert_ffn(tile_expert, n_active.astype(jnp.int32), xs, wg, wu, wd, tm_rows, tf)
    return _combine(pos, ys, x1, info, gate, gf, final)


def _pad_heads(w, widths, total=LANES):
    k = w.shape[0]
    per = sum(widths)
    w = w.reshape(k, HEADS, per)
    w = jnp.pad(w, ((0, 0), (0, 0), (0, total - per)))
    return w.reshape(k, HEADS * total)


def _rope_swap(w_rope):
    half = w_rope.shape[-1] // 2
    return jnp.concatenate([-w_rope[..., half:], w_rope[..., :half]], axis=-1)


def _layer_weights(w_in, w_uq, w_ukv):
    o = IN_OFFS
    seg = lambda i: w_in[:, o[i]:o[i + 1]]
    d = w_in.shape[0]
    zeros = lambda n: jnp.zeros((d, n), jnp.float32)
    w_kr = seg(2)
    t1 = jnp.concatenate([zeros(MLA_NOPE), w_kr, zeros(32)], axis=1)
    t2 = jnp.concatenate([zeros(MLA_NOPE), _rope_swap(w_kr), zeros(32)], axis=1)
    wz = jnp.concatenate([seg(6)] * 3 + [zeros(LANES - 3 * HEADS)], axis=1)
    wa = jnp.concatenate([seg(0), seg(1), t1, t2, wz,
                          _pad_heads(seg(3) * FOX_SCALE, (FOX_DIM,)), _pad_heads(seg(4), (FOX_DIM,)),
                          seg(5)], axis=1)
    wg = jnp.concatenate([seg(7), seg(8)], axis=1)
    uq = w_uq.reshape(MLA_Q_LORA, HEADS, MLA_NOPE + MLA_ROPE)
    nope, rope = uq[..., :MLA_NOPE], uq[..., MLA_NOPE:]
    pad = jnp.zeros((MLA_Q_LORA, HEADS, LANES - MLA_NOPE - MLA_ROPE), jnp.float32)
    w1 = jnp.concatenate([nope, rope, pad], axis=-1).reshape(MLA_Q_LORA, HEAD_W)
    w2 = jnp.concatenate([jnp.zeros_like(nope), _rope_swap(rope), pad], axis=-1).reshape(MLA_Q_LORA, HEAD_W)
    ukv = w_ukv.reshape(MLA_KV_LORA, HEADS, MLA_NOPE + MLA_V)
    wk = _pad_heads(ukv[..., :MLA_NOPE].reshape(MLA_KV_LORA, -1), (MLA_NOPE,))
    wv = ukv[..., MLA_NOPE:].reshape(MLA_KV_LORA, HEADS * MLA_V)
    return tuple(_bf(t) for t in (wa, wg, w1, w2, wk, wv))


def _fox_extras():
    p = np.zeros((LANES, 2 * HEAD_W), np.float32)
    ones = np.zeros((1, 2 * HEAD_W), np.float32)
    for hd in range(HEADS):
        base = hd * LANES + FOX_DIM
        for piece in range(3):
            p[piece * HEADS + hd, base + piece] = 1.0
            ones[0, base + 3 + piece] = 1.0
            ones[0, HEAD_W + base + piece] = 1.0
            p[piece * HEADS + hd, HEAD_W + base + 3 + piece] = -1.0
    return jnp.asarray(p, jnp.bfloat16), jnp.asarray(ones, jnp.float32)


def kernel(x, c, positions, ada_w, ada_b, norm_mix_g, norm_ffn_g, w_in, b_forget, q_norm_g, w_uq,
           kv_norm_g, w_ukv, w_branch_mla, w_branch_fox, w_out, dense_w_gate, dense_w_up, dense_w_down,
           moe_w_router, moe_w_gate, moe_w_up, moe_w_down, final_norm_g):
    bsz, seq, d = x.shape
    depth = w_in.shape[0]
    tm = min(512, seq)
    tm_ffn = min(1024, seq)
    tm_tok = min(256, seq)
    tf = 512

    mod = _adaln(c, ada_w.reshape(depth * 2, d, 3 * d), ada_b.reshape(depth * 2, 1, 3 * d))
    mod = mod.reshape(depth, 2, bsz, 1, 3, d)
    cos, sin = _rope_tables(positions)
    pmat, ones = _fox_extras()
    row = lambda v: v.reshape(1, -1)

    for l in range(depth):
        wa, wg, w1, w2, wk, wv = _layer_weights(w_in[l], w_uq[l], w_ukv[l])
        mod1 = [mod[l, 0, :, :, i] for i in range(3)]
        mod2 = [mod[l, 1, :, :, i] for i in range(3)]
        bfr = jnp.concatenate([b_forget[l]] * 3 + [jnp.zeros((LANES - 3 * HEADS,), jnp.float32)])[None, :]
        qm, km, vmt, fq, fk, fvt = _mix_in(
            x, mod1[0], mod1[1], row(norm_mix_g[l]), wa, row(q_norm_g[l]), row(kv_norm_g[l]),
            w1, w2, wk, wv, cos, sin, bfr, pmat, ones, tm)
        y_mla = _attention(qm, km, vmt, tm)
        y_fox = _attention(fq, fk, fvt, tm)
        moe = l % 2 == 1
        if moe:
            wr = jnp.pad(moe_w_router[l // 2], ((0, 0), (0, LANES - N_EXPERTS)))
            wr_hi = _bf(wr)
            wr = jnp.stack([wr_hi, _bf(wr - wr_hi.astype(jnp.float32))])
            fg, fu, fd = moe_w_gate[l // 2], moe_w_up[l // 2], moe_w_down[l // 2]
        else:
            wr = jnp.zeros((2, d, LANES), jnp.bfloat16)
            fg, fu, fd = dense_w_gate[l // 2][None], dense_w_up[l // 2][None], dense_w_down[l // 2][None]
        x1, h2, info, counts = _mix_out(x, y_mla, y_fox, mod1, row(norm_mix_g[l]), mod2, row(norm_ffn_g[l]),
                                        wg, _bf(w_branch_mla[l]), _bf(w_branch_fox[l]), _bf(w_out[l]), wr, moe, tm)
        if moe:
            x = _moe(x1, h2, info, counts, mod2[2], row(final_norm_g), _bf(fg), _bf(fu), _bf(fd),
                     l == depth - 1, tm_tok, tm_ffn, tf)
        else:
            x = _ffn(x1, h2, mod2[2], _bf(fg), _bf(fu), _bf(fd), tm_ffn, tf)
    return x
```

```python
import functools

import jax
import jax.numpy as jnp
import numpy as np
from jax import lax
from jax.experimental import pallas as pl
from jax.experimental.pallas import tpu as pltpu

D_MODEL = 1024
HEADS = 8
MLA_NOPE = 64
MLA_ROPE = 32
MLA_V = 64
MLA_Q_LORA = 256
MLA_KV_LORA = 128
ROPE_THETA = 10000.0
FOX_DIM = 64
D_FF = 3584
N_EXPERTS = 8
RMS_EPS = 1e-6
LANES = 128
HEAD_W = HEADS * LANES
LOG2E = 1.4426950408889634
MLA_SCALE = float((MLA_NOPE + MLA_ROPE) ** -0.5)
FOX_SCALE = 0.125
NEG_BIG = -1e30
RC = 32
VMEM_LIMIT = 56 * 1024 * 1024

IN_WIDTHS = (MLA_Q_LORA, MLA_KV_LORA, MLA_ROPE, 512, 512, 512, HEADS, D_MODEL, D_MODEL)
IN_OFFS = tuple(int(v) for v in np.cumsum((0,) + IN_WIDTHS))

A_CQ = (0, 256)
A_CKV = (256, 384)
A_T1 = (384, 512)
A_T2 = (512, 640)
A_Z = (640, 768)
A_FQ = (768, 1792)
A_FK = (1792, 2816)
A_FV = (2816, 3328)
A_COLS = 3328


def _cparams(n_axes):
    return pltpu.CompilerParams(dimension_semantics=("arbitrary",) * n_axes,
                                vmem_limit_bytes=VMEM_LIMIT)


def _dot(a, b):
    return jnp.dot(a, b, preferred_element_type=jnp.float32)


def _dot_nt(a, b):
    return lax.dot_general(a, b, (((1,), (1,)), ((), ())), preferred_element_type=jnp.float32)


def _bf(x):
    return x.astype(jnp.bfloat16)


def _split3(x):
    hi = _bf(x)
    r = x - hi.astype(jnp.float32)
    mid = _bf(r)
    lo = _bf(r - mid.astype(jnp.float32))
    return hi, mid, lo


def _rms(x, g):
    return x * lax.rsqrt(jnp.mean(x * x, axis=-1, keepdims=True) + RMS_EPS) * g


def _sigmoid(x):
    return 1.0 / (1.0 + jnp.exp(-x))


def _adaln_kernel(c_ref, w_ref, b_ref, o_ref):
    c = c_ref[...]
    a = c * _sigmoid(c)
    a_hi, a_mid, _ = _split3(a)
    w = w_ref[0]
    w_hi = _bf(w)
    w_lo = _bf(w - w_hi.astype(jnp.float32))
    o_ref[0] = _dot(a_hi, w_hi) + _dot(a_mid, w_hi) + _dot(a_hi, w_lo) + b_ref[0]


def _adaln(c, w, b):
    g, d, n = w.shape
    bsz = c.shape[0]
    tn = 1024
    return pl.pallas_call(
        _adaln_kernel,
        grid=(g, n // tn),
        in_specs=[pl.BlockSpec((bsz, d), lambda i, j: (0, 0)),
                  pl.BlockSpec((1, d, tn), lambda i, j: (i, 0, j)),
                  pl.BlockSpec((1, 1, tn), lambda i, j: (i, 0, j))],
        out_specs=pl.BlockSpec((1, bsz, tn), lambda i, j: (i, 0, j)),
        out_shape=jax.ShapeDtypeStruct((g, bsz, n), jnp.float32),
        compiler_params=_cparams(2),
        name="adaln",
    )(c, w, b)


def _rope_kernel(pos_ref, freq_ref, mask_ref, cos_ref, sin_ref):
    ang = pos_ref[0] * freq_ref[...]
    rope = mask_ref[0:1, :]
    nope = mask_ref[1:2, :]
    cos_ref[0] = jnp.cos(ang) * rope + nope
    sin_ref[0] = jnp.sin(ang) * rope


def _rope_tables(positions):
    bsz, seq = positions.shape
    inv = ROPE_THETA ** (-jnp.arange(0, MLA_ROPE, 2, dtype=jnp.float32) / MLA_ROPE)
    zeros = jnp.zeros((MLA_NOPE,), jnp.float32)
    freq = jnp.concatenate([zeros, inv, inv, jnp.zeros((32,), jnp.float32)])[None, :]
    lane = np.arange(LANES)
    mask = jnp.asarray(np.stack([(lane >= 64) & (lane < 96), lane < 64]).astype(np.float32))
    mask = jnp.concatenate([mask, jnp.zeros((6, LANES), jnp.float32)], axis=0)
    pos = positions.astype(jnp.float32)[..., None]
    ts = min(seq, 1024)
    return pl.pallas_call(
        _rope_kernel,
        grid=(bsz, seq // ts),
        in_specs=[pl.BlockSpec((1, ts, 1), lambda b, i: (b, i, 0)),
                  pl.BlockSpec((1, LANES), lambda b, i: (0, 0)),
                  pl.BlockSpec((8, LANES), lambda b, i: (0, 0))],
        out_specs=[pl.BlockSpec((1, ts, LANES), lambda b, i: (b, i, 0))] * 2,
        out_shape=[jax.ShapeDtypeStruct((bsz, seq, LANES), jnp.float32)] * 2,
        compiler_params=_cparams(2),
        name="rope_tables",
    )(pos, freq, mask)


def _mix_in_kernel(x_ref, shift_ref, scale_ref, g_ref, wa_ref, qg_ref, kvg_ref, w1_ref, w2_ref,
                   wk_ref, wv_ref, cos_ref, sin_ref, bf_ref, p_ref, ones_ref,
                   qm_ref, km_ref, vmt_ref, fq_ref, fk_ref, fvt_ref, carry_ref):
    tm = x_ref.shape[1]
    x = x_ref[0]
    h = _rms(x, g_ref[...]) * (1.0 + scale_ref[0]) + shift_ref[0]
    hb = _bf(h)

    def proj(cols):
        return _dot(hb, wa_ref[:, cols[0]:cols[1]])

    cos = cos_ref[0]
    sin = sin_ref[0]

    cqn = _bf(_rms(proj(A_CQ), qg_ref[...]))
    q1 = _dot(cqn, w1_ref[...])
    q2 = _dot(cqn, w2_ref[...])
    ckvn = _bf(_rms(proj(A_CKV), kvg_ref[...]))
    kp = _dot(ckvn, wk_ref[...])
    krope = proj(A_T1) * cos + proj(A_T2) * sin
    for hd in range(HEADS):
        sl = slice(hd * LANES, (hd + 1) * LANES)
        qm_ref[0, :, sl] = _bf((q1[:, sl] * cos + q2[:, sl] * sin) * (MLA_SCALE * LOG2E))
        km_ref[0, :, sl] = _bf(kp[:, sl] + krope)
    vmt_ref[0, 0] = _bf(_dot(ckvn, wv_ref[...]).T)

    @pl.when(pl.program_id(1) == 0)
    def _():
        carry_ref[...] = jnp.zeros_like(carry_ref)

    z = proj(A_Z) + bf_ref[...]
    lf = jnp.minimum(z, 0.0) - jnp.log(1.0 + jnp.exp(-jnp.abs(z)))
    row = lax.broadcasted_iota(jnp.int32, (tm, tm), 0)
    col = lax.broadcasted_iota(jnp.int32, (tm, tm), 1)
    tri = jnp.where(row >= col, 1.0, 0.0).astype(jnp.bfloat16)
    hi, mid, lo = _split3(lf)
    cum = _dot(tri, hi) + _dot(tri, mid) + _dot(tri, lo) + carry_ref[0:1, :]
    carry_ref[0:1, :] = cum[tm - 1:tm, :]
    chi, cmid, clo = _split3(cum * LOG2E)
    lane = lax.broadcasted_iota(jnp.int32, (tm, LANES), 1)
    pieces = jnp.where(lane < 8, chi, jnp.where(lane < 16, cmid, clo))
    extras = _dot(pieces, p_ref[...]) + ones_ref[...]
    fq_ref[0] = _bf(proj(A_FQ) * LOG2E + extras[:, :HEAD_W])
    fk_ref[0] = _bf(proj(A_FK) + extras[:, HEAD_W:])
    fvt_ref[0, 0] = _bf(proj(A_FV).T)


def _mix_in(x, shift, scale, g, wa, qg, kvg, w1, w2, wk, wv, cos, sin, bfr, pmat, ones, tm):
    bsz, seq, d = x.shape
    full = lambda shape: pl.BlockSpec(shape, lambda b, i: (0,) * len(shape))
    tok = lambda w: pl.BlockSpec((1, tm, w), lambda b, i: (b, i, 0))
    per_b = pl.BlockSpec((1, 1, d), lambda b, i: (b, 0, 0))
    tr = pl.BlockSpec((1, 1, 512, tm), lambda b, i: (b, i, 0, 0))
    act = jax.ShapeDtypeStruct((bsz, seq, HEAD_W), jnp.bfloat16)
    act_t = jax.ShapeDtypeStruct((bsz, seq // tm, 512, tm), jnp.bfloat16)
    return pl.pallas_call(
        _mix_in_kernel,
        grid=(bsz, seq // tm),
        in_specs=[tok(d), per_b, per_b, full((1, d)), full(wa.shape), full(qg.shape), full(kvg.shape),
                  full(w1.shape), full(w2.shape), full(wk.shape), full(wv.shape),
                  tok(LANES), tok(LANES), full(bfr.shape), full(pmat.shape), full(ones.shape)],
        out_specs=[tok(HEAD_W), tok(HEAD_W), tr, tok(HEAD_W), tok(HEAD_W), tr],
        out_shape=[act, act, act_t, act, act, act_t],
        scratch_shapes=[pltpu.VMEM((8, LANES), jnp.float32)],
        compiler_params=_cparams(2),
        name="mix_in",
    )(x, shift, scale, g, wa, qg, kvg, w1, w2, wk, wv, cos, sin, bfr, pmat, ones)


def _attn_kernel(q_ref, k_ref, vt_ref, o_ref, m_ref, acc_ref, s_ref, p_ref):
    qi = pl.program_id(2)
    tq = q_ref.shape[1]
    tk = vt_ref.shape[3]
    dv = acc_ref.shape[1] - 16

    m_ref[...] = jnp.full_like(m_ref, NEG_BIG)
    acc_ref[...] = jnp.zeros_like(acc_ref)

    def block(ki, masked):
        rows_k = pl.ds(pl.multiple_of(ki * tk, tk), tk)
        for hh in range(2):
            sl = slice(hh * LANES, (hh + 1) * LANES)
            s = _dot_nt(k_ref[0, rows_k, sl], q_ref[0, :, sl])
            if masked:
                kpos = lax.broadcasted_iota(jnp.int32, (tk, tq), 0)
                qpos = lax.broadcasted_iota(jnp.int32, (tk, tq), 1)
                s = jnp.where(kpos <= qpos, s, NEG_BIG)
            s_ref[hh] = s
        for hh in range(2):
            m_old = m_ref[hh, 0:1, :]
            m_new = jnp.maximum(m_old, jnp.max(s_ref[hh], axis=0, keepdims=True))
            alpha = jnp.exp2(m_old - m_new)
            m_ref[hh, 0:1, :] = m_new
            mb = jnp.broadcast_to(m_new, (RC, tq))
            for c in range(tk // RC):
                rows = slice(c * RC, (c + 1) * RC)
                p_ref[hh, rows, :] = _bf(jnp.exp2(s_ref[hh, rows, :] - mb))
            va = jnp.concatenate([vt_ref[0, ki, hh * dv:(hh + 1) * dv, :], jnp.ones((16, tk), jnp.bfloat16)], axis=0)
            acc_ref[hh] = alpha * acc_ref[hh] + _dot(va, p_ref[hh])

    def body(ki, carry):
        block(ki, False)
        return carry

    lax.fori_loop(0, qi, body, 0)
    block(qi, True)
    o0 = acc_ref[0, :dv, :] / acc_ref[0, dv:dv + 1, :]
    o1 = acc_ref[1, :dv, :] / acc_ref[1, dv:dv + 1, :]
    o_ref[0] = _bf(jnp.concatenate([o0, o1], axis=0).T)


def _attention(q, k, vt, t):
    bsz, seq, _ = q.shape
    dv = vt.shape[2] // HEADS
    n = seq // t
    return pl.pallas_call(
        _attn_kernel,
        grid=(bsz, HEADS // 2, n),
        in_specs=[pl.BlockSpec((1, t, 2 * LANES), lambda b, j, qi: (b, qi, j)),
                  pl.BlockSpec((1, seq, 2 * LANES), lambda b, j, qi: (b, 0, j)),
                  pl.BlockSpec((1, n, 2 * dv, t), lambda b, j, qi: (b, 0, j, 0))],
        out_specs=pl.BlockSpec((1, t, 2 * dv), lambda b, j, qi: (b, qi, j)),
        out_shape=jax.ShapeDtypeStruct((bsz, seq, HEADS * dv), jnp.bfloat16),
        scratch_shapes=[pltpu.VMEM((2, 8, t), jnp.float32),
                        pltpu.VMEM((2, dv + 16, t), jnp.float32),
                        pltpu.VMEM((2, t, t), jnp.float32),
                        pltpu.VMEM((2, t, t), jnp.bfloat16)],
        compiler_params=_cparams(3),
        name="attention",
    )(q, k, vt)


R_MASK = 8
R_I1 = 16
R_W1 = 18


def _mix_out_kernel(x_ref, ym_ref, yf_ref, shift1_ref, scale1_ref, gate1_ref, g1_ref,
                    shift2_ref, scale2_ref, g2_ref, wg_ref, wm_ref, wf_ref, wo_ref, wr_ref,
                    x1_ref, h2_ref, info_ref, cnt_ref, *, route):
    x = x_ref[0]
    hb = _bf(_rms(x, g1_ref[...]) * (1.0 + scale1_ref[0]) + shift1_ref[0])
    ga = _dot(hb, wg_ref[:, :D_MODEL])
    gb = _dot(hb, wg_ref[:, D_MODEL:])
    merged = _sigmoid(ga) * _dot(ym_ref[0], wm_ref[...]) + _sigmoid(gb) * _dot(yf_ref[0], wf_ref[...])
    x1 = x + gate1_ref[0] * _dot(_bf(merged), wo_ref[...])
    x1_ref[0] = x1
    h2 = _rms(x1, g2_ref[...]) * (1.0 + scale2_ref[0]) + shift2_ref[0]
    h2_ref[0] = h2.astype(h2_ref.dtype)

    @pl.when((pl.program_id(0) == 0) & (pl.program_id(1) == 0))
    def _():
        cnt_ref[...] = jnp.zeros_like(cnt_ref)

    if route:
        h_hi, h_mid, _ = _split3(h2)
        lg = _dot(h_hi, wr_ref[0]) + _dot(h_mid, wr_ref[0]) + _dot(h_hi, wr_ref[1])
        lane_i = lax.broadcasted_iota(jnp.int32, lg.shape, 1)
        lane = lane_i.astype(jnp.float32)
        lg = jnp.where(lane_i < N_EXPERTS, lg, -jnp.inf)
        m1 = jnp.max(lg, axis=1, keepdims=True)
        i1 = jnp.min(jnp.where(lg == m1, lane, float(LANES)), axis=1, keepdims=True)
        lg2 = jnp.where(lane == i1, -jnp.inf, lg)
        m2 = jnp.max(lg2, axis=1, keepdims=True)
        i2 = jnp.min(jnp.where(lg2 == m2, lane, float(LANES)), axis=1, keepdims=True)
        e2 = jnp.exp(m2 - m1)
        w1 = 1.0 / (1.0 + e2)
        mask = jnp.where(lane - R_MASK == i1, 1.0, jnp.where(lane - R_MASK == i2, 1.0, 0.0))
        info = mask + jnp.where(lane_i == R_I1, i1, 0.0) + jnp.where(lane_i == R_I1 + 1, i2, 0.0)
        info = info + jnp.where(lane_i == R_W1, w1, 0.0) + jnp.where(lane_i == R_W1 + 1, e2 * w1, 0.0)
        info_ref[0] = info
        cnt_ref[0:1, :] += jnp.sum(mask, axis=0, keepdims=True)
    else:
        info_ref[0] = jnp.zeros_like(info_ref[0])


def _mix_out(x, ym, yf, mod1, g1, mod2, g2, wg, wm, wf, wo, wr, route, tm):
    bsz, seq, d = x.shape
    full = lambda shape: pl.BlockSpec(shape, lambda b, i: (0,) * len(shape))
    tok = lambda w: pl.BlockSpec((1, tm, w), lambda b, i: (b, i, 0))
    per_b = pl.BlockSpec((1, 1, d), lambda b, i: (b, 0, 0))
    return pl.pallas_call(
        functools.partial(_mix_out_kernel, route=route),
        grid=(bsz, seq // tm),
        in_specs=[tok(d), tok(512), tok(512), per_b, per_b, per_b, full((1, d)),
                  per_b, per_b, full((1, d)), full(wg.shape), full(wm.shape), full(wf.shape),
                  full(wo.shape), full(wr.shape)],
        out_specs=[tok(d), tok(d), tok(LANES), full((8, LANES))],
        out_shape=[jax.ShapeDtypeStruct((bsz, seq, d), jnp.float32),
                   jax.ShapeDtypeStruct((bsz, seq, d), jnp.float32 if route else jnp.bfloat16),
                   jax.ShapeDtypeStruct((bsz, seq, LANES), jnp.float32),
                   jax.ShapeDtypeStruct((8, LANES), jnp.float32)],
        compiler_params=_cparams(2),
        name="mix_out",
    )(x, ym, yf, mod1[0], mod1[1], mod1[2], g1, mod2[0], mod2[1], g2, wg, wm, wf, wo, wr)


def _swiglu_chunk(h, wg_ref, wu_ref, wd_ref):
    gp = _dot(h, wg_ref[0])
    up = _dot(h, wu_ref[0])
    return _dot(_bf(gp * _sigmoid(gp) * up), wd_ref[0])


def _ffn_kernel(x_ref, h_ref, gate_ref, wg_ref, wu_ref, wd_ref, o_ref, acc_ref):
    j = pl.program_id(2)

    @pl.when(j == 0)
    def _():
        acc_ref[...] = jnp.zeros_like(acc_ref)

    acc_ref[...] += _swiglu_chunk(h_ref[0], wg_ref, wu_ref, wd_ref)

    @pl.when(j == pl.num_programs(2) - 1)
    def _():
        o_ref[0] = x_ref[0] + gate_ref[0] * acc_ref[...]


def _ffn(x, h, gate, wg, wu, wd, tm, tf):
    bsz, seq, d = x.shape
    dff = wg.shape[2]
    tok = pl.BlockSpec((1, tm, d), lambda b, i, j: (b, i, 0))
    return pl.pallas_call(
        _ffn_kernel,
        grid=(bsz, seq // tm, dff // tf),
        in_specs=[tok, tok, pl.BlockSpec((1, 1, d), lambda b, i, j: (b, 0, 0)),
                  pl.BlockSpec((1, d, tf), lambda b, i, j: (0, 0, j)),
                  pl.BlockSpec((1, d, tf), lambda b, i, j: (0, 0, j)),
                  pl.BlockSpec((1, tf, d), lambda b, i, j: (0, j, 0))],
        out_specs=tok,
        out_shape=jax.ShapeDtypeStruct((bsz, seq, d), jnp.float32),
        scratch_shapes=[pltpu.VMEM((tm, d), jnp.float32)],
        compiler_params=_cparams(3),
        name="ffn",
    )(x, h, gate, wg, wu, wd)


def _route_pos_kernel(info_ref, offs_ref, pos_ref, carry_ref):
    tm = info_ref.shape[1]

    @pl.when((pl.program_id(0) == 0) & (pl.program_id(1) == 0))
    def _():
        carry_ref[...] = jnp.zeros_like(carry_ref)

    info = info_ref[0]
    lane_i = lax.broadcasted_iota(jnp.int32, info.shape, 1)
    mask = jnp.where(lane_i >= R_MASK, jnp.where(lane_i < R_MASK + N_EXPERTS, info, 0.0), 0.0)
    row = lax.broadcasted_iota(jnp.int32, (tm, tm), 0)
    col = lax.broadcasted_iota(jnp.int32, (tm, tm), 1)
    tri = jnp.where(row > col, 1.0, 0.0).astype(jnp.bfloat16)
    posv = _dot(tri, _bf(mask)) + carry_ref[0:1, :] + offs_ref[...]
    carry_ref[0:1, :] += jnp.sum(mask, axis=0, keepdims=True)
    lane = lane_i.astype(jnp.float32) - R_MASK
    pa = jnp.sum(jnp.where(lane == info[:, R_I1:R_I1 + 1], posv, 0.0), axis=1, keepdims=True)
    pb = jnp.sum(jnp.where(lane == info[:, R_I1 + 1:R_I1 + 2], posv, 0.0), axis=1, keepdims=True)
    both = jnp.where(lane_i == 0, pa, jnp.where(lane_i == 1, pb, 0.0))
    pos_ref[0] = both.T[0:8, :].astype(jnp.int32)


def _route_pos(info, offs, tm):
    bsz, seq, _ = info.shape
    nt = seq // tm
    return pl.pallas_call(
        _route_pos_kernel,
        grid=(bsz, nt),
        in_specs=[pl.BlockSpec((1, tm, LANES), lambda b, i: (b, i, 0)),
                  pl.BlockSpec((1, LANES), lambda b, i: (0, 0))],
        out_specs=pl.BlockSpec((1, 8, tm), lambda b, i: (b * nt + i, 0, 0)),
        out_shape=jax.ShapeDtypeStruct((bsz * nt, 8, tm), jnp.int32),
        scratch_shapes=[pltpu.VMEM((8, LANES), jnp.float32)],
        compiler_params=_cparams(2),
        name="route_pos",
    )(info, offs)


def _row_copy(src, s_row, dst, d_row, sem):
    return pltpu.make_async_copy(src.at[pl.ds(s_row, 1)], dst.at[pl.ds(d_row, 1)], sem)


def _dispatch_kernel(pos_ref, h_ref, zeros_ref, xs_ref, sem):
    del zeros_ref
    tm = pos_ref.shape[2]

    def issue(r, c):
        _row_copy(h_ref, r, xs_ref, pos_ref[0, 0, r], sem).start()
        _row_copy(h_ref, r, xs_ref, pos_ref[0, 1, r], sem).start()
        return c

    def drain(r, c):
        _row_copy(h_ref, r, xs_ref, pos_ref[0, 0, r], sem).wait()
        _row_copy(h_ref, r, xs_ref, pos_ref[0, 1, r], sem).wait()
        return c

    lax.fori_loop(0, tm, issue, 0)
    lax.fori_loop(0, tm, drain, 0)


def _dispatch(pos, h, rows):
    n, _, tm = pos.shape
    d = h.shape[1]
    return pl.pallas_call(
        _dispatch_kernel,
        grid=(n,),
        in_specs=[pl.BlockSpec((1, 8, tm), lambda i: (i, 0, 0), memory_space=pltpu.SMEM),
                  pl.BlockSpec((tm, d), lambda i: (i, 0)),
                  pl.BlockSpec(memory_space=pl.ANY)],
        out_specs=pl.BlockSpec(memory_space=pl.ANY),
        out_shape=jax.ShapeDtypeStruct((rows, d), h.dtype),
        scratch_shapes=[pltpu.SemaphoreType.DMA(())],
        input_output_aliases={2: 0},
        compiler_params=_cparams(1),
        name="dispatch",
    )(pos, h, jnp.zeros((rows, d), h.dtype))


def _expert_ffn_kernel(te_ref, na_ref, xs_ref, wg_ref, wu_ref, wd_ref, ys_ref, acc_ref):
    del te_ref
    i = pl.program_id(0)
    j = pl.program_id(1)

    @pl.when(j == 0)
    def _():
        acc_ref[...] = jnp.zeros_like(acc_ref)

    @pl.when(i < na_ref[0])
    def _():
        acc_ref[...] += _swiglu_chunk(_bf(xs_ref[...]), wg_ref, wu_ref, wd_ref)

    @pl.when(j == pl.num_programs(1) - 1)
    def _():
        ys_ref[...] = acc_ref[...]


def _expert_ffn(tile_expert, n_active, xs, wg, wu, wd, tm, tf):
    rows, d = xs.shape
    dff = wg.shape[2]
    grid_spec = pltpu.PrefetchScalarGridSpec(
        num_scalar_prefetch=2,
        grid=(rows // tm, dff // tf),
        in_specs=[pl.BlockSpec((tm, d), lambda i, j, te, na: (i, 0)),
                  pl.BlockSpec((1, d, tf), lambda i, j, te, na: (te[i], 0, j)),
                  pl.BlockSpec((1, d, tf), lambda i, j, te, na: (te[i], 0, j)),
                  pl.BlockSpec((1, tf, d), lambda i, j, te, na: (te[i], j, 0))],
        out_specs=pl.BlockSpec((tm, d), lambda i, j, te, na: (i, 0)),
        scratch_shapes=[pltpu.VMEM((tm, d), jnp.float32)])
    return pl.pallas_call(
        _expert_ffn_kernel,
        grid_spec=grid_spec,
        out_shape=jax.ShapeDtypeStruct((rows, d), jnp.float32),
        compiler_params=_cparams(2),
        name="expert_ffn",
    )(tile_expert, n_active, xs, wg, wu, wd)


def _combine_kernel(pos_ref, ys_ref, x_ref, info_ref, gate_ref, gf_ref, o_ref, buf_a, buf_b, sem, *, final):
    tm = x_ref.shape[1]

    def issue(r, c):
        _row_copy(ys_ref, pos_ref[0, 0, r], buf_a, r, sem.at[0]).start()
        _row_copy(ys_ref, pos_ref[0, 1, r], buf_b, r, sem.at[1]).start()
        return c

    def drain(r, c):
        _row_copy(ys_ref, pos_ref[0, 0, r], buf_a, r, sem.at[0]).wait()
        _row_copy(ys_ref, pos_ref[0, 1, r], buf_b, r, sem.at[1]).wait()
        return c

    lax.fori_loop(0, tm, issue, 0)
    lax.fori_loop(0, tm, drain, 0)
    info = info_ref[0]
    f = info[:, R_W1:R_W1 + 1] * buf_a[...] + info[:, R_W1 + 1:R_W1 + 2] * buf_b[...]
    y = x_ref[0] + gate_ref[0] * f
    if final:
        y = _rms(y, gf_ref[...])
    o_ref[0] = y


def _combine(pos, ys, x, info, gate, gf, final):
    bsz, seq, d = x.shape
    tm = pos.shape[2]
    nt = seq // tm
    tok = lambda w: pl.BlockSpec((1, tm, w), lambda b, i: (b, i, 0))
    return pl.pallas_call(
        functools.partial(_combine_kernel, final=final),
        grid=(bsz, nt),
        in_specs=[pl.BlockSpec((1, 8, tm), lambda b, i: (b * nt + i, 0, 0), memory_space=pltpu.SMEM),
                  pl.BlockSpec(memory_space=pl.ANY),
                  tok(d), tok(LANES),
                  pl.BlockSpec((1, 1, d), lambda b, i: (b, 0, 0)),
                  pl.BlockSpec((1, d), lambda b, i: (0, 0))],
        out_specs=tok(d),
        out_shape=jax.ShapeDtypeStruct((bsz, seq, d), jnp.float32),
        scratch_shapes=[pltpu.VMEM((tm, d), jnp.float32), pltpu.VMEM((tm, d), jnp.float32),
                        pltpu.SemaphoreType.DMA((2,))],
        compiler_params=_cparams(2),
        name="combine",
    )(pos, ys, x, info, gate, gf)


def _moe(x1, h2, info, counts, gate, gf, wg, wu, wd, final, tm_tok, tm_rows, tf):
    bsz, seq, d = x1.shape
    n_tok = bsz * seq
    n_tiles = 2 * n_tok // tm_rows + N_EXPERTS
    cnt = counts[0, R_MASK:R_MASK + N_EXPERTS].astype(jnp.int32)
    tiles = (cnt + tm_rows - 1) // tm_rows
    ends = jnp.cumsum(tiles)
    starts = ((ends - tiles) * tm_rows).astype(jnp.float32)
    offs = jnp.zeros((1, LANES), jnp.float32).at[0, R_MASK:R_MASK + N_EXPERTS].set(starts)
    n_active = ends[-1:]
    tile_ids = jnp.minimum(jnp.arange(n_tiles, dtype=jnp.int32), n_active - 1)
    tile_expert = jnp.sum(ends[None, :] <= tile_ids[:, None], axis=1).astype(jnp.int32)
    pos = _route_pos(info, offs, tm_tok)
    xs = _dispatch(pos, h2.reshape(n_tok, d), n_tiles * tm_rows)
    ys = _expert_ffn(tile_expert, n_active.astype(jnp.int32), xs, wg, wu, wd, tm_rows, tf)
    return _combine(pos, ys, x1, info, gate, gf, final)


def _pad_heads(w, widths, total=LANES):
    k = w.shape[0]
    per = sum(widths)
    w = w.reshape(k, HEADS, per)
    w = jnp.pad(w, ((0, 0), (0, 0), (0, total - per)))
    return w.reshape(k, HEADS * total)


def _rope_swap(w_rope):
    half = w_rope.shape[-1] // 2
    return jnp.concatenate([-w_rope[..., half:], w_rope[..., :half]], axis=-1)


def _layer_weights(w_in, w_uq, w_ukv):
    o = IN_OFFS
    seg = lambda i: w_in[:, o[i]:o[i + 1]]
    d = w_in.shape[0]
    zeros = lambda n: jnp.zeros((d, n), jnp.float32)
    w_kr = seg(2)
    t1 = jnp.concatenate([zeros(MLA_NOPE), w_kr, zeros(32)], axis=1)
    t2 = jnp.concatenate([zeros(MLA_NOPE), _rope_swap(w_kr), zeros(32)], axis=1)
    wz = jnp.concatenate([seg(6)] * 3 + [zeros(LANES - 3 * HEADS)], axis=1)
    wa = jnp.concatenate([seg(0), seg(1), t1, t2, wz,
                          _pad_heads(seg(3) * FOX_SCALE, (FOX_DIM,)), _pad_heads(seg(4), (FOX_DIM,)),
                          seg(5)], axis=1)
    wg = jnp.concatenate([seg(7), seg(8)], axis=1)
    uq = w_uq.reshape(MLA_Q_LORA, HEADS, MLA_NOPE + MLA_ROPE)
    nope, rope = uq[..., :MLA_NOPE], uq[..., MLA_NOPE:]
    pad = jnp.zeros((MLA_Q_LORA, HEADS, LANES - MLA_NOPE - MLA_ROPE), jnp.float32)
    w1 = jnp.concatenate([nope, rope, pad], axis=-1).reshape(MLA_Q_LORA, HEAD_W)
    w2 = jnp.concatenate([jnp.zeros_like(nope), _rope_swap(rope), pad], axis=-1).reshape(MLA_Q_LORA, HEAD_W)
    ukv = w_ukv.reshape(MLA_KV_LORA, HEADS, MLA_NOPE + MLA_V)
    wk = _pad_heads(ukv[..., :MLA_NOPE].reshape(MLA_KV_LORA, -1), (MLA_NOPE,))
    wv = ukv[..., MLA_NOPE:].reshape(MLA_KV_LORA, HEADS * MLA_V)
    return tuple(_bf(t) for t in (wa, wg, w1, w2, wk, wv))


def _fox_extras():
    p = np.zeros((LANES, 2 * HEAD_W), np.float32)
    ones = np.zeros((1, 2 * HEAD_W), np.float32)
    for hd in range(HEADS):
        base = hd * LANES + FOX_DIM
        for piece in range(3):
            p[piece * HEADS + hd, base + piece] = 1.0
            ones[0, base + 3 + piece] = 1.0
            ones[0, HEAD_W + base + piece] = 1.0
            p[piece * HEADS + hd, HEAD_W + base + 3 + piece] = -1.0
    return jnp.asarray(p, jnp.bfloat16), jnp.asarray(ones, jnp.float32)


def kernel(x, c, positions, ada_w, ada_b, norm_mix_g, norm_ffn_g, w_in, b_forget, q_norm_g, w_uq,
           kv_norm_g, w_ukv, w_branch_mla, w_branch_fox, w_out, dense_w_gate, dense_w_up, dense_w_down,
           moe_w_router, moe_w_gate, moe_w_up, moe_w_down, final_norm_g):
    bsz, seq, d = x.shape
    depth = w_in.shape[0]
    tm = min(512, seq)
    tm_ffn = min(1024, seq)
    tm_tok = min(256, seq)
    tf = 512

    mod = _adaln(c, ada_w.reshape(depth * 2, d, 3 * d), ada_b.reshape(depth * 2, 1, 3 * d))
    mod = mod.reshape(depth, 2, bsz, 1, 3, d)
    cos, sin = _rope_tables(positions)
    pmat, ones = _fox_extras()
    row = lambda v: v.reshape(1, -1)

    for l in range(depth):
        wa, wg, w1, w2, wk, wv = _layer_weights(w_in[l], w_uq[l], w_ukv[l])
        mod1 = [mod[l, 0, :, :, i] for i in range(3)]
        mod2 = [mod[l, 1, :, :, i] for i in range(3)]
        bfr = jnp.concatenate([b_forget[l]] * 3 + [jnp.zeros((LANES - 3 * HEADS,), jnp.float32)])[None, :]
        qm, km, vmt, fq, fk, fvt = _mix_in(
            x, mod1[0], mod1[1], row(norm_mix_g[l]), wa, row(q_norm_g[l]), row(kv_norm_g[l]),
            w1, w2, wk, wv, cos, sin, bfr, pmat, ones, tm)
        y_mla = _attention(qm, km, vmt, tm)
        y_fox = _attention(fq, fk, fvt, tm)
        moe = l % 2 == 1
        if moe:
            wr = jnp.pad(moe_w_router[l // 2], ((0, 0), (0, LANES - N_EXPERTS)))
            wr_hi = _bf(wr)
            wr = jnp.stack([wr_hi, _bf(wr - wr_hi.astype(jnp.float32))])
            fg, fu, fd = moe_w_gate[l // 2], moe_w_up[l // 2], moe_w_down[l // 2]
        else:
            wr = jnp.zeros((2, d, LANES), jnp.bfloat16)
            fg, fu, fd = dense_w_gate[l // 2][None], dense_w_up[l // 2][None], dense_w_down[l // 2][None]
        x1, h2, info, counts = _mix_out(x, y_mla, y_fox, mod1, row(norm_mix_g[l]), mod2, row(norm_ffn_g[l]),
                                        wg, _bf(w_branch_mla[l]), _bf(w_branch_fox[l]), _bf(w_out[l]), wr, moe, tm)
        if moe:
            x = _moe(x1, h2, info, counts, mod2[2], row(final_norm_g), _bf(fg), _bf(fu), _bf(fd),
                     l == depth - 1, tm_tok, tm_ffn, tf)
        else:
            x = _ffn(x1, h2, mod2[2], _bf(fg), _bf(fu), _bf(fd), tm_ffn, tf)
    return x
```

```python
import functools

import jax
import jax.numpy as jnp
import numpy as np
from jax import lax
from jax.experimental import pallas as pl
from jax.experimental.pallas import tpu as pltpu

D_MODEL = 1024
HEADS = 8
MLA_NOPE = 64
MLA_ROPE = 32
MLA_V = 64
MLA_Q_LORA = 256
MLA_KV_LORA = 128
ROPE_THETA = 10000.0
FOX_DIM = 64
D_FF = 3584
N_EXPERTS = 8
RMS_EPS = 1e-6
LANES = 128
HEAD_W = HEADS * LANES
LOG2E = 1.4426950408889634
MLA_SCALE = float((MLA_NOPE + MLA_ROPE) ** -0.5)
FOX_SCALE = 0.125
NEG_BIG = -1e30
RC = 32
VMEM_LIMIT = 56 * 1024 * 1024

IN_WIDTHS = (MLA_Q_LORA, MLA_KV_LORA, MLA_ROPE, 512, 512, 512, HEADS, D_MODEL, D_MODEL)
IN_OFFS = tuple(int(v) for v in np.cumsum((0,) + IN_WIDTHS))

A_CQ = (0, 256)
A_CKV = (256, 384)
A_T1 = (384, 512)
A_T2 = (512, 640)
A_Z = (640, 768)
A_FQ = (768, 1792)
A_FK = (1792, 2816)
A_FV = (2816, 3328)
A_COLS = 3328


def _cparams(n_axes):
    return pltpu.CompilerParams(dimension_semantics=("arbitrary",) * n_axes,
                                vmem_limit_bytes=VMEM_LIMIT)


def _dot(a, b):
    return jnp.dot(a, b, preferred_element_type=jnp.float32)


def _dot_nt(a, b):
    return lax.dot_general(a, b, (((1,), (1,)), ((), ())), preferred_element_type=jnp.float32)


def _bf(x):
    return x.astype(jnp.bfloat16)


def _split3(x):
    hi = _bf(x)
    r = x - hi.astype(jnp.float32)
    mid = _bf(r)
    lo = _bf(r - mid.astype(jnp.float32))
    return hi, mid, lo


def _rms(x, g):
    return x * lax.rsqrt(jnp.mean(x * x, axis=-1, keepdims=True) + RMS_EPS) * g


def _sigmoid(x):
    return 1.0 / (1.0 + jnp.exp(-x))


def _adaln_kernel(c_ref, w_ref, b_ref, o_ref):
    c = c_ref[...]
    a = c * _sigmoid(c)
    a_hi, a_mid, _ = _split3(a)
    w = w_ref[0]
    w_hi = _bf(w)
    w_lo = _bf(w - w_hi.astype(jnp.float32))
    o_ref[0] = _dot(a_hi, w_hi) + _dot(a_mid, w_hi) + _dot(a_hi, w_lo) + b_ref[0]


def _adaln(c, w, b):
    g, d, n = w.shape
    bsz = c.shape[0]
    tn = 1024
    return pl.pallas_call(
        _adaln_kernel,
        grid=(g, n // tn),
        in_specs=[pl.BlockSpec((bsz, d), lambda i, j: (0, 0)),
                  pl.BlockSpec((1, d, tn), lambda i, j: (i, 0, j)),
                  pl.BlockSpec((1, 1, tn), lambda i, j: (i, 0, j))],
        out_specs=pl.BlockSpec((1, bsz, tn), lambda i, j: (i, 0, j)),
        out_shape=jax.ShapeDtypeStruct((g, bsz, n), jnp.float32),
        compiler_params=_cparams(2),
        name="adaln",
    )(c, w, b)


def _rope_kernel(pos_ref, freq_ref, mask_ref, cos_ref, sin_ref):
    ang = pos_ref[0] * freq_ref[...]
    rope = mask_ref[0:1, :]
    nope = mask_ref[1:2, :]
    cos_ref[0] = jnp.cos(ang) * rope + nope
    sin_ref[0] = jnp.sin(ang) * rope


def _rope_tables(positions):
    bsz, seq = positions.shape
    inv = ROPE_THETA ** (-jnp.arange(0, MLA_ROPE, 2, dtype=jnp.float32) / MLA_ROPE)
    zeros = jnp.zeros((MLA_NOPE,), jnp.float32)
    freq = jnp.concatenate([zeros, inv, inv, jnp.zeros((32,), jnp.float32)])[None, :]
    lane = np.arange(LANES)
    mask = jnp.asarray(np.stack([(lane >= 64) & (lane < 96), lane < 64]).astype(np.float32))
    mask = jnp.concatenate([mask, jnp.zeros((6, LANES), jnp.float32)], axis=0)
    pos = positions.astype(jnp.float32)[..., None]
    ts = min(seq, 1024)
    return pl.pallas_call(
        _rope_kernel,
        grid=(bsz, seq // ts),
        in_specs=[pl.BlockSpec((1, ts, 1), lambda b, i: (b, i, 0)),
                  pl.BlockSpec((1, LANES), lambda b, i: (0, 0)),
                  pl.BlockSpec((8, LANES), lambda b, i: (0, 0))],
        out_specs=[pl.BlockSpec((1, ts, LANES), lambda b, i: (b, i, 0))] * 2,
        out_shape=[jax.ShapeDtypeStruct((bsz, seq, LANES), jnp.float32)] * 2,
        compiler_params=_cparams(2),
        name="rope_tables",
    )(pos, freq, mask)


def _mix_in_kernel(x_ref, shift_ref, scale_ref, g_ref, wa_ref, qg_ref, kvg_ref, w1_ref, w2_ref,
                   wk_ref, wv_ref, cos_ref, sin_ref, bf_ref, p_ref, ones_ref,
                   qm_ref, km_ref, vmt_ref, fq_ref, fk_ref, fvt_ref, carry_ref):
    tm = x_ref.shape[1]
    x = x_ref[0]
    h = _rms(x, g_ref[...]) * (1.0 + scale_ref[0]) + shift_ref[0]
    hb = _bf(h)

    def proj(cols):
        return _dot(hb, wa_ref[:, cols[0]:cols[1]])

    cos = cos_ref[0]
    sin = sin_ref[0]

    cqn = _bf(_rms(proj(A_CQ), qg_ref[...]))
    q1 = _dot(cqn, w1_ref[...])
    q2 = _dot(cqn, w2_ref[...])
    ckvn = _bf(_rms(proj(A_CKV), kvg_ref[...]))
    kp = _dot(ckvn, wk_ref[...])
    krope = proj(A_T1) * cos + proj(A_T2) * sin
    for hd in range(HEADS):
        sl = slice(hd * LANES, (hd + 1) * LANES)
        qm_ref[0, :, sl] = _bf((q1[:, sl] * cos + q2[:, sl] * sin) * (MLA_SCALE * LOG2E))
        km_ref[0, :, sl] = _bf(kp[:, sl] + krope)
    vmt_ref[0, 0] = _bf(_dot(ckvn, wv_ref[...]).T)

    @pl.when(pl.program_id(1) == 0)
    def _():
        carry_ref[...] = jnp.zeros_like(carry_ref)

    z = proj(A_Z) + bf_ref[...]
    lf = jnp.minimum(z, 0.0) - jnp.log(1.0 + jnp.exp(-jnp.abs(z)))
    row = lax.broadcasted_iota(jnp.int32, (tm, tm), 0)
    col = lax.broadcasted_iota(jnp.int32, (tm, tm), 1)
    tri = jnp.where(row >= col, 1.0, 0.0).astype(jnp.bfloat16)
    hi, mid, lo = _split3(lf)
    cum = _dot(tri, hi) + _dot(tri, mid) + _dot(tri, lo) + carry_ref[0:1, :]
    carry_ref[0:1, :] = cum[tm - 1:tm, :]
    chi, cmid, clo = _split3(cum * LOG2E)
    lane = lax.broadcasted_iota(jnp.int32, (tm, LANES), 1)
    pieces = jnp.where(lane < 8, chi, jnp.where(lane < 16, cmid, clo))
    extras = _dot(pieces, p_ref[...]) + ones_ref[...]
    fq_ref[0] = _bf(proj(A_FQ) * LOG2E + extras[:, :HEAD_W])
    fk_ref[0] = _bf(proj(A_FK) + extras[:, HEAD_W:])
    fvt_ref[0, 0] = _bf(proj(A_FV).T)


def _mix_in(x, shift, scale, g, wa, qg, kvg, w1, w2, wk, wv, cos, sin, bfr, pmat, ones, tm):
    bsz, seq, d = x.shape
    full = lambda shape: pl.BlockSpec(shape, lambda b, i: (0,) * len(shape))
    tok = lambda w: pl.BlockSpec((1, tm, w), lambda b, i: (b, i, 0))
    per_b = pl.BlockSpec((1, 1, d), lambda b, i: (b, 0, 0))
    tr = pl.BlockSpec((1, 1, 512, tm), lambda b, i: (b, i, 0, 0))
    act = jax.ShapeDtypeStruct((bsz, seq, HEAD_W), jnp.bfloat16)
    act_t = jax.ShapeDtypeStruct((bsz, seq // tm, 512, tm), jnp.bfloat16)
    return pl.pallas_call(
        _mix_in_kernel,
        grid=(bsz, seq // tm),
        in_specs=[tok(d), per_b, per_b, full((1, d)), full(wa.shape), full(qg.shape), full(kvg.shape),
                  full(w1.shape), full(w2.shape), full(wk.shape), full(wv.shape),
                  tok(LANES), tok(LANES), full(bfr.shape), full(pmat.shape), full(ones.shape)],
        out_specs=[tok(HEAD_W), tok(HEAD_W), tr, tok(HEAD_W), tok(HEAD_W), tr],
        out_shape=[act, act, act_t, act, act, act_t],
        scratch_shapes=[pltpu.VMEM((8, LANES), jnp.float32)],
        compiler_params=_cparams(2),
        name="mix_in",
    )(x, shift, scale, g, wa, qg, kvg, w1, w2, wk, wv, cos, sin, bfr, pmat, ones)


def _attn_kernel(q_ref, k_ref, vt_ref, o_ref, m_ref, acc_ref, s_ref, p_ref):
    qi = pl.program_id(2)
    tq = q_ref.shape[1]
    tk = vt_ref.shape[3]
    dv = acc_ref.shape[1] - 16

    m_ref[...] = jnp.full_like(m_ref, NEG_BIG)
    acc_ref[...] = jnp.zeros_like(acc_ref)

    def block(ki, masked):
        rows_k = pl.ds(pl.multiple_of(ki * tk, tk), tk)
        for hh in range(2):
            sl = slice(hh * LANES, (hh + 1) * LANES)
            s = _dot_nt(k_ref[0, rows_k, sl], q_ref[0, :, sl])
            if masked:
                kpos = lax.broadcasted_iota(jnp.int32, (tk, tq), 0)
                qpos = lax.broadcasted_iota(jnp.int32, (tk, tq), 1)
                s = jnp.where(kpos <= qpos, s, NEG_BIG)
            s_ref[hh] = s
        for hh in range(2):
            m_old = m_ref[hh, 0:1, :]
            m_new = jnp.maximum(m_old, jnp.max(s_ref[hh], axis=0, keepdims=True))
            alpha = jnp.exp2(m_old - m_new)
            m_ref[hh, 0:1, :] = m_new
            mb = jnp.broadcast_to(m_new, (RC, tq))
            for c in range(tk // RC):
                rows = slice(c * RC, (c + 1) * RC)
                p_ref[hh, rows, :] = _bf(jnp.exp2(s_ref[hh, rows, :] - mb))
            va = jnp.concatenate([vt_ref[0, ki, hh * dv:(hh + 1) * dv, :], jnp.ones((16, tk), jnp.bfloat16)], axis=0)
            acc_ref[hh] = alpha * acc_ref[hh] + _dot(va, p_ref[hh])

    def body(ki, carry):
        block(ki, False)
        return carry

    lax.fori_loop(0, qi, body, 0)
    block(qi, True)
    o0 = acc_ref[0, :dv, :] / acc_ref[0, dv:dv + 1, :]
    o1 = acc_ref[1, :dv, :] / acc_ref[1, dv:dv + 1, :]
    o_ref[0] = _bf(jnp.concatenate([o0, o1], axis=0).T)


def _attention(q, k, vt, t):
    bsz, seq, _ = q.shape
    dv = vt.shape[2] // HEADS
    n = seq // t
    return pl.pallas_call(
        _attn_kernel,
        grid=(bsz, HEADS // 2, n),
        in_specs=[pl.BlockSpec((1, t, 2 * LANES), lambda b, j, qi: (b, qi, j)),
                  pl.BlockSpec((1, seq, 2 * LANES), lambda b, j, qi: (b, 0, j)),
                  pl.BlockSpec((1, n, 2 * dv, t), lambda b, j, qi: (b, 0, j, 0))],
        out_specs=pl.BlockSpec((1, t, 2 * dv), lambda b, j, qi: (b, qi, j)),
        out_shape=jax.ShapeDtypeStruct((bsz, seq, HEADS * dv), jnp.bfloat16),
        scratch_shapes=[pltpu.VMEM((2, 8, t), jnp.float32),
                        pltpu.VMEM((2, dv + 16, t), jnp.float32),
                        pltpu.VMEM((2, t, t), jnp.float32),
                        pltpu.VMEM((2, t, t), jnp.bfloat16)],
        compiler_params=_cparams(3),
        name="attention",
    )(q, k, vt)


R_MASK = 8
R_I1 = 16
R_W1 = 18


def _mix_out_kernel(x_ref, ym_ref, yf_ref, shift1_ref, scale1_ref, gate1_ref, g1_ref,
                    shift2_ref, scale2_ref, g2_ref, wg_ref, wm_ref, wf_ref, wo_ref, wr_ref,
                    x1_ref, h2_ref, info_ref, cnt_ref, *, route):
    x = x_ref[0]
    hb = _bf(_rms(x, g1_ref[...]) * (1.0 + scale1_ref[0]) + shift1_ref[0])
    ga = _dot(hb, wg_ref[:, :D_MODEL])
    gb = _dot(hb, wg_ref[:, D_MODEL:])
    merged = _sigmoid(ga) * _dot(ym_ref[0], wm_ref[...]) + _sigmoid(gb) * _dot(yf_ref[0], wf_ref[...])
    x1 = x + gate1_ref[0] * _dot(_bf(merged), wo_ref[...])
    x1_ref[0] = x1
    h2 = _rms(x1, g2_ref[...]) * (1.0 + scale2_ref[0]) + shift2_ref[0]
    h2_ref[0] = h2.astype(h2_ref.dtype)

    @pl.when((pl.program_id(0) == 0) & (pl.program_id(1) == 0))
    def _():
        cnt_ref[...] = jnp.zeros_like(cnt_ref)

    if route:
        h_hi, h_mid, _ = _split3(h2)
        lg = _dot(h_hi, wr_ref[0]) + _dot(h_mid, wr_ref[0]) + _dot(h_hi, wr_ref[1])
        lane_i = lax.broadcasted_iota(jnp.int32, lg.shape, 1)
        lane = lane_i.astype(jnp.float32)
        lg = jnp.where(lane_i < N_EXPERTS, lg, -jnp.inf)
        m1 = jnp.max(lg, axis=1, keepdims=True)
        i1 = jnp.min(jnp.where(lg == m1, lane, float(LANES)), axis=1, keepdims=True)
        lg2 = jnp.where(lane == i1, -jnp.inf, lg)
        m2 = jnp.max(lg2, axis=1, keepdims=True)
        i2 = jnp.min(jnp.where(lg2 == m2, lane, float(LANES)), axis=1, keepdims=True)
        e2 = jnp.exp(m2 - m1)
        w1 = 1.0 / (1.0 + e2)
        mask = jnp.where(lane - R_MASK == i1, 1.0, jnp.where(lane - R_MASK == i2, 1.0, 0.0))
        info = mask + jnp.where(lane_i == R_I1, i1, 0.0) + jnp.where(lane_i == R_I1 + 1, i2, 0.0)
        info = info + jnp.where(lane_i == R_W1, w1, 0.0) + jnp.where(lane_i == R_W1 + 1, e2 * w1, 0.0)
        info_ref[0] = info
        cnt_ref[0:1, :] += jnp.sum(mask, axis=0, keepdims=True)
    else:
        info_ref[0] = jnp.zeros_like(info_ref[0])


def _mix_out(x, ym, yf, mod1, g1, mod2, g2, wg, wm, wf, wo, wr, route, tm):
    bsz, seq, d = x.shape
    full = lambda shape: pl.BlockSpec(shape, lambda b, i: (0,) * len(shape))
    tok = lambda w: pl.BlockSpec((1, tm, w), lambda b, i: (b, i, 0))
    per_b = pl.BlockSpec((1, 1, d), lambda b, i: (b, 0, 0))
    return pl.pallas_call(
        functools.partial(_mix_out_kernel, route=route),
        grid=(bsz, seq // tm),
        in_specs=[tok(d), tok(512), tok(512), per_b, per_b, per_b, full((1, d)),
                  per_b, per_b, full((1, d)), full(wg.shape), full(wm.shape), full(wf.shape),
                  full(wo.shape), full(wr.shape)],
        out_specs=[tok(d), tok(d), tok(LANES), full((8, LANES))],
        out_shape=[jax.ShapeDtypeStruct((bsz, seq, d), jnp.float32),
                   jax.ShapeDtypeStruct((bsz, seq, d), jnp.float32 if route else jnp.bfloat16),
                   jax.ShapeDtypeStruct((bsz, seq, LANES), jnp.float32),
                   jax.ShapeDtypeStruct((8, LANES), jnp.float32)],
        compiler_params=_cparams(2),
        name="mix_out",
    )(x, ym, yf, mod1[0], mod1[1], mod1[2], g1, mod2[0], mod2[1], g2, wg, wm, wf, wo, wr)


def _swiglu_chunk(h, wg_ref, wu_ref, wd_ref):
    gp = _dot(h, wg_ref[0])
    up = _dot(h, wu_ref[0])
    return _dot(_bf(gp * _sigmoid(gp) * up), wd_ref[0])


def _ffn_kernel(x_ref, h_ref, gate_ref, wg_ref, wu_ref, wd_ref, o_ref, acc_ref):
    j = pl.program_id(2)

    @pl.when(j == 0)
    def _():
        acc_ref[...] = jnp.zeros_like(acc_ref)

    acc_ref[...] += _swiglu_chunk(h_ref[0], wg_ref, wu_ref, wd_ref)

    @pl.when(j == pl.num_programs(2) - 1)
    def _():
        o_ref[0] = x_ref[0] + gate_ref[0] * acc_ref[...]


def _ffn(x, h, gate, wg, wu, wd, tm, tf):
    bsz, seq, d = x.shape
    dff = wg.shape[2]
    tok = pl.BlockSpec((1, tm, d), lambda b, i, j: (b, i, 0))
    return pl.pallas_call(
        _ffn_kernel,
        grid=(bsz, seq // tm, dff // tf),
        in_specs=[tok, tok, pl.BlockSpec((1, 1, d), lambda b, i, j: (b, 0, 0)),
                  pl.BlockSpec((1, d, tf), lambda b, i, j: (0, 0, j)),
                  pl.BlockSpec((1, d, tf), lambda b, i, j: (0, 0, j)),
                  pl.BlockSpec((1, tf, d), lambda b, i, j: (0, j, 0))],
        out_specs=tok,
        out_shape=jax.ShapeDtypeStruct((bsz, seq, d), jnp.float32),
        scratch_shapes=[pltpu.VMEM((tm, d), jnp.float32)],
        compiler_params=_cparams(3),
        name="ffn",
    )(x, h, gate, wg, wu, wd)


def _route_pos_kernel(info_ref, offs_ref, pos_ref, carry_ref):
    tm = info_ref.shape[1]

    @pl.when((pl.program_id(0) == 0) & (pl.program_id(1) == 0))
    def _():
        carry_ref[...] = jnp.zeros_like(carry_ref)

    info = info_ref[0]
    lane_i = lax.broadcasted_iota(jnp.int32, info.shape, 1)
    mask = jnp.where(lane_i >= R_MASK, jnp.where(lane_i < R_MASK + N_EXPERTS, info, 0.0), 0.0)
    row = lax.broadcasted_iota(jnp.int32, (tm, tm), 0)
    col = lax.broadcasted_iota(jnp.int32, (tm, tm), 1)
    tri = jnp.where(row > col, 1.0, 0.0).astype(jnp.bfloat16)
    posv = _dot(tri, _bf(mask)) + carry_ref[0:1, :] + offs_ref[...]
    carry_ref[0:1, :] += jnp.sum(mask, axis=0, keepdims=True)
    lane = lane_i.astype(jnp.float32) - R_MASK
    pa = jnp.sum(jnp.where(lane == info[:, R_I1:R_I1 + 1], posv, 0.0), axis=1, keepdims=True)
    pb = jnp.sum(jnp.where(lane == info[:, R_I1 + 1:R_I1 + 2], posv, 0.0), axis=1, keepdims=True)
    both = jnp.where(lane_i == 0, pa, jnp.where(lane_i == 1, pb, 0.0))
    pos_ref[0] = both.T[0:8, :].astype(jnp.int32)


def _route_pos(info, offs, tm):
    bsz, seq, _ = info.shape
    nt = seq // tm
    return pl.pallas_call(
        _route_pos_kernel,
        grid=(bsz, nt),
        in_specs=[pl.BlockSpec((1, tm, LANES), lambda b, i: (b, i, 0)),
                  pl.BlockSpec((1, LANES), lambda b, i: (0, 0))],
        out_specs=pl.BlockSpec((1, 8, tm), lambda b, i: (b * nt + i, 0, 0)),
        out_shape=jax.ShapeDtypeStruct((bsz * nt, 8, tm), jnp.int32),
        scratch_shapes=[pltpu.VMEM((8, LANES), jnp.float32)],
        compiler_params=_cparams(2),
        name="route_pos",
    )(info, offs)


def _row_copy(src, s_row, dst, d_row, sem):
    return pltpu.make_async_copy(src.at[pl.ds(s_row, 1)], dst.at[pl.ds(d_row, 1)], sem)


def _dispatch_kernel(pos_ref, h_ref, zeros_ref, xs_ref, sem):
    del zeros_ref
    tm = pos_ref.shape[2]

    def issue(r, c):
        _row_copy(h_ref, r, xs_ref, pos_ref[0, 0, r], sem).start(priority=0)
        _row_copy(h_ref, r, xs_ref, pos_ref[0, 1, r], sem).start(priority=1)
        return c

    def drain(r, c):
        _row_copy(h_ref, r, xs_ref, pos_ref[0, 0, r], sem).wait()
        _row_copy(h_ref, r, xs_ref, pos_ref[0, 1, r], sem).wait()
        return c

    lax.fori_loop(0, tm, issue, 0)
    lax.fori_loop(0, tm, drain, 0)


def _dispatch(pos, h, rows):
    n, _, tm = pos.shape
    d = h.shape[1]
    return pl.pallas_call(
        _dispatch_kernel,
        grid=(n,),
        in_specs=[pl.BlockSpec((1, 8, tm), lambda i: (i, 0, 0), memory_space=pltpu.SMEM),
                  pl.BlockSpec((tm, d), lambda i: (i, 0)),
                  pl.BlockSpec(memory_space=pl.ANY)],
        out_specs=pl.BlockSpec(memory_space=pl.ANY),
        out_shape=jax.ShapeDtypeStruct((rows, d), h.dtype),
        scratch_shapes=[pltpu.SemaphoreType.DMA(())],
        input_output_aliases={2: 0},
        compiler_params=_cparams(1),
        name="dispatch",
    )(pos, h, jnp.zeros((rows, d), h.dtype))


def _expert_ffn_kernel(te_ref, na_ref, xs_ref, wg_ref, wu_ref, wd_ref, ys_ref, acc_ref):
    del te_ref
    i = pl.program_id(0)
    j = pl.program_id(1)

    @pl.when(j == 0)
    def _():
        acc_ref[...] = jnp.zeros_like(acc_ref)

    @pl.when(i < na_ref[0])
    def _():
        acc_ref[...] += _swiglu_chunk(_bf(xs_ref[...]), wg_ref, wu_ref, wd_ref)

    @pl.when(j == pl.num_programs(1) - 1)
    def _():
        ys_ref[...] = acc_ref[...]


def _expert_ffn(tile_expert, n_active, xs, wg, wu, wd, tm, tf):
    rows, d = xs.shape
    dff = wg.shape[2]
    grid_spec = pltpu.PrefetchScalarGridSpec(
        num_scalar_prefetch=2,
        grid=(rows // tm, dff // tf),
        in_specs=[pl.BlockSpec((tm, d), lambda i, j, te, na: (i, 0)),
                  pl.BlockSpec((1, d, tf), lambda i, j, te, na: (te[i], 0, j)),
                  pl.BlockSpec((1, d, tf), lambda i, j, te, na: (te[i], 0, j)),
                  pl.BlockSpec((1, tf, d), lambda i, j, te, na: (te[i], j, 0))],
        out_specs=pl.BlockSpec((tm, d), lambda i, j, te, na: (i, 0)),
        scratch_shapes=[pltpu.VMEM((tm, d), jnp.float32)])
    return pl.pallas_call(
        _expert_ffn_kernel,
        grid_spec=grid_spec,
        out_shape=jax.ShapeDtypeStruct((rows, d), jnp.float32),
        compiler_params=_cparams(2),
        name="expert_ffn",
    )(tile_expert, n_active, xs, wg, wu, wd)


def _combine_kernel(pos_ref, ys_ref, x_ref, info_ref, gate_ref, gf_ref, o_ref, buf_a, buf_b, sem, *, final):
    tm = x_ref.shape[1]

    def issue(r, c):
        _row_copy(ys_ref, pos_ref[0, 0, r], buf_a, r, sem.at[0]).start(priority=0)
        _row_copy(ys_ref, pos_ref[0, 1, r], buf_b, r, sem.at[1]).start(priority=1)
        return c

    def drain(r, c):
        _row_copy(ys_ref, pos_ref[0, 0, r], buf_a, r, sem.at[0]).wait()
        _row_copy(ys_ref, pos_ref[0, 1, r], buf_b, r, sem.at[1]).wait()
        return c

    lax.fori_loop(0, tm, issue, 0)
    lax.fori_loop(0, tm, drain, 0)
    info = info_ref[0]
    f = info[:, R_W1:R_W1 + 1] * buf_a[...] + info[:, R_W1 + 1:R_W1 + 2] * buf_b[...]
    y = x_ref[0] + gate_ref[0] * f
    if final:
        y = _rms(y, gf_ref[...])
    o_ref[0] = y


def _combine(pos, ys, x, info, gate, gf, final):
    bsz, seq, d = x.shape
    tm = pos.shape[2]
    nt = seq // tm
    tok = lambda w: pl.BlockSpec((1, tm, w), lambda b, i: (b, i, 0))
    return pl.pallas_call(
        functools.partial(_combine_kernel, final=final),
        grid=(bsz, nt),
        in_specs=[pl.BlockSpec((1, 8, tm), lambda b, i: (b * nt + i, 0, 0), memory_space=pltpu.SMEM),
                  pl.BlockSpec(memory_space=pl.ANY),
                  tok(d), tok(LANES),
                  pl.BlockSpec((1, 1, d), lambda b, i: (b, 0, 0)),
                  pl.BlockSpec((1, d), lambda b, i: (0, 0))],
        out_specs=tok(d),
        out_shape=jax.ShapeDtypeStruct((bsz, seq, d), jnp.float32),
        scratch_shapes=[pltpu.VMEM((tm, d), jnp.float32), pltpu.VMEM((tm, d), jnp.float32),
                        pltpu.SemaphoreType.DMA((2,))],
        compiler_params=_cparams(2),
        name="combine",
    )(pos, ys, x, info, gate, gf)


def _moe(x1, h2, info, counts, gate, gf, wg, wu, wd, final, tm_tok, tm_rows, tf):
    bsz, seq, d = x1.shape
    n_tok = bsz * seq
    n_tiles = 2 * n_tok // tm_rows + N_EXPERTS
    cnt = counts[0, R_MASK:R_MASK + N_EXPERTS].astype(jnp.int32)
    tiles = (cnt + tm_rows - 1) // tm_rows
    ends = jnp.cumsum(tiles)
    starts = ((ends - tiles) * tm_rows).astype(jnp.float32)
    offs = jnp.zeros((1, LANES), jnp.float32).at[0, R_MASK:R_MASK + N_EXPERTS].set(starts)
    n_active = ends[-1:]
    tile_ids = jnp.minimum(jnp.arange(n_tiles, dtype=jnp.int32), n_active - 1)
    tile_expert = jnp.sum(ends[None, :] <= tile_ids[:, None], axis=1).astype(jnp.int32)
    pos = _route_pos(info, offs, tm_tok)
    xs = _dispatch(pos, h2.reshape(n_tok, d), n_tiles * tm_rows)
    ys = _expert_ffn(tile_expert, n_active.astype(jnp.int32), xs, wg, wu, wd, tm_rows, tf)
    return _combine(pos, ys, x1, info, gate, gf, final)


def _pad_heads(w, widths, total=LANES):
    k = w.shape[0]
    per = sum(widths)
    w = w.reshape(k, HEADS, per)
    w = jnp.pad(w, ((0, 0), (0, 0), (0, total - per)))
    return w.reshape(k, HEADS * total)


def _rope_swap(w_rope):
    half = w_rope.shape[-1] // 2
    return jnp.concatenate([-w_rope[..., half:], w_rope[..., :half]], axis=-1)


def _layer_weights(w_in, w_uq, w_ukv):
    o = IN_OFFS
    seg = lambda i: w_in[:, o[i]:o[i + 1]]
    d = w_in.shape[0]
    zeros = lambda n: jnp.zeros((d, n), jnp.float32)
    w_kr = seg(2)
    t1 = jnp.concatenate([zeros(MLA_NOPE), w_kr, zeros(32)], axis=1)
    t2 = jnp.concatenate([zeros(MLA_NOPE), _rope_swap(w_kr), zeros(32)], axis=1)
    wz = jnp.concatenate([seg(6)] * 3 + [zeros(LANES - 3 * HEADS)], axis=1)
    wa = jnp.concatenate([seg(0), seg(1), t1, t2, wz,
                          _pad_heads(seg(3) * FOX_SCALE, (FOX_DIM,)), _pad_heads(seg(4), (FOX_DIM,)),
                          seg(5)], axis=1)
    wg = jnp.concatenate([seg(7), seg(8)], axis=1)
    uq = w_uq.reshape(MLA_Q_LORA, HEADS, MLA_NOPE + MLA_ROPE)
    nope, rope = uq[..., :MLA_NOPE], uq[..., MLA_NOPE:]
    pad = jnp.zeros((MLA_Q_LORA, HEADS, LANES - MLA_NOPE - MLA_ROPE), jnp.float32)
    w1 = jnp.concatenate([nope, rope, pad], axis=-1).reshape(MLA_Q_LORA, HEAD_W)
    w2 = jnp.concatenate([jnp.zeros_like(nope), _rope_swap(rope), pad], axis=-1).reshape(MLA_Q_LORA, HEAD_W)
    ukv = w_ukv.reshape(MLA_KV_LORA, HEADS, MLA_NOPE + MLA_V)
    wk = _pad_heads(ukv[..., :MLA_NOPE].reshape(MLA_KV_LORA, -1), (MLA_NOPE,))
    wv = ukv[..., MLA_NOPE:].reshape(MLA_KV_LORA, HEADS * MLA_V)
    return tuple(_bf(t) for t in (wa, wg, w1, w2, wk, wv))


def _fox_extras():
    p = np.zeros((LANES, 2 * HEAD_W), np.float32)
    ones = np.zeros((1, 2 * HEAD_W), np.float32)
    for hd in range(HEADS):
        base = hd * LANES + FOX_DIM
        for piece in range(3):
            p[piece * HEADS + hd, base + piece] = 1.0
            ones[0, base + 3 + piece] = 1.0
            ones[0, HEAD_W + base + piece] = 1.0
            p[piece * HEADS + hd, HEAD_W + base + 3 + piece] = -1.0
    return jnp.asarray(p, jnp.bfloat16), jnp.asarray(ones, jnp.float32)


def kernel(x, c, positions, ada_w, ada_b, norm_mix_g, norm_ffn_g, w_in, b_forget, q_norm_g, w_uq,
           kv_norm_g, w_ukv, w_branch_mla, w_branch_fox, w_out, dense_w_gate, dense_w_up, dense_w_down,
           moe_w_router, moe_w_gate, moe_w_up, moe_w_down, final_norm_g):
    bsz, seq, d = x.shape
    depth = w_in.shape[0]
    tm = min(512, seq)
    tm_ffn = min(1024, seq)
    tm_tok = min(256, seq)
    tf = 512

    mod = _adaln(c, ada_w.reshape(depth * 2, d, 3 * d), ada_b.reshape(depth * 2, 1, 3 * d))
    mod = mod.reshape(depth, 2, bsz, 1, 3, d)
    cos, sin = _rope_tables(positions)
    pmat, ones = _fox_extras()
    row = lambda v: v.reshape(1, -1)

    for l in range(depth):
        wa, wg, w1, w2, wk, wv = _layer_weights(w_in[l], w_uq[l], w_ukv[l])
        mod1 = [mod[l, 0, :, :, i] for i in range(3)]
        mod2 = [mod[l, 1, :, :, i] for i in range(3)]
        bfr = jnp.concatenate([b_forget[l]] * 3 + [jnp.zeros((LANES - 3 * HEADS,), jnp.float32)])[None, :]
        qm, km, vmt, fq, fk, fvt = _mix_in(
            x, mod1[0], mod1[1], row(norm_mix_g[l]), wa, row(q_norm_g[l]), row(kv_norm_g[l]),
            w1, w2, wk, wv, cos, sin, bfr, pmat, ones, tm)
        y_mla = _attention(qm, km, vmt, tm)
        y_fox = _attention(fq, fk, fvt, tm)
        moe = l % 2 == 1
        if moe:
            wr = jnp.pad(moe_w_router[l // 2], ((0, 0), (0, LANES - N_EXPERTS)))
            wr_hi = _bf(wr)
            wr = jnp.stack([wr_hi, _bf(wr - wr_hi.astype(jnp.float32))])
            fg, fu, fd = moe_w_gate[l // 2], moe_w_up[l // 2], moe_w_down[l // 2]
        else:
            wr = jnp.zeros((2, d, LANES), jnp.bfloat16)
            fg, fu, fd = dense_w_gate[l // 2][None], dense_w_up[l // 2][None], dense_w_down[l // 2][None]
        x1, h2, info, counts = _mix_out(x, y_mla, y_fox, mod1, row(norm_mix_g[l]), mod2, row(norm_ffn_g[l]),
                                        wg, _bf(w_branch_mla[l]), _bf(w_branch_fox[l]), _bf(w_out[l]), wr, moe, tm)
        if moe:
            x = _moe(x1, h2, info, counts, mod2[2], row(final_norm_g), _bf(fg), _bf(fu), _bf(fd),
                     l == depth - 1, tm_tok, tm_ffn, tf)
        else:
            x = _ffn(x1, h2, mod2[2], _bf(fg), _bf(fu), _bf(fd), tm_ffn, tf)
    return x
```
